```python
import math
import jax, jax.numpy as jnp
from jax import lax
import numpy as np

D_MODEL = 1024
BATCH = 2
SEQ = 8192
DEPTH = 2

N_A_LAYERS = DEPTH // 2
N_B_LAYERS = DEPTH - N_A_LAYERS
SB_HEADS = 16
SB_HEAD_DIM = D_MODEL // SB_HEADS
DIFF_HEAD_DIM = 64
DIFF_HEADS = D_MODEL // (2 * DIFF_HEAD_DIM)
DIFF_V_DIM = 2 * DIFF_HEAD_DIM
ROPE_THETA = 10000.0
Q_BLOCK = 128
EPS = 1e-6

kernel_name = "yoco_stickbreak_diffattn_hybrid"


def rms_norm(x, g):
    xf = x.astype(jnp.float32)
    y = xf * lax.rsqrt(jnp.mean(xf * xf, axis=-1, keepdims=True) + EPS) * g.astype(jnp.float32)
    return y.astype(x.dtype)


def rope_tables(positions, dim):
    inv_freq = ROPE_THETA ** (-jnp.arange(0, dim, 2, dtype=jnp.float32) / dim)
    ang = positions.astype(jnp.float32)[..., None] * inv_freq
    return jnp.cos(ang), jnp.sin(ang)


def apply_rope(t, cos, sin):
    tf = t.astype(jnp.float32)
    t1, t2 = jnp.split(tf, 2, axis=-1)
    out = jnp.concatenate([t1 * cos - t2 * sin, t2 * cos + t1 * sin], axis=-1)
    return out.astype(t.dtype)


def stick_breaking_attention(q, k, v):
    B, H, S, D = q.shape
    nb = S // Q_BLOCK
    scale = 1.0 / math.sqrt(D)
    qb = q.reshape(B, H, nb, Q_BLOCK, D).transpose(2, 0, 1, 3, 4)
    starts = jnp.arange(nb, dtype=jnp.int32) * Q_BLOCK
    key_pos = jnp.arange(S, dtype=jnp.int32)
    kf = k.astype(jnp.float32)
    vf = v.astype(jnp.float32)

    def block(args):
        qblk, t0 = args
        z = jnp.einsum('bhqd,bhkd->bhqk', qblk.astype(jnp.float32), kf) * scale
        tq = t0 + jnp.arange(Q_BLOCK, dtype=jnp.int32)
        strict = key_pos[None, :] < tq[:, None]
        log_1m = jnp.where(strict, jax.nn.log_sigmoid(-z), 0.0)
        suffix = lax.cumsum(log_1m, axis=3, reverse=True) - log_1m
        a = jnp.where(strict, jnp.exp(jax.nn.log_sigmoid(z) + suffix), 0.0)
        return jnp.einsum('bhqk,bhkd->bhqd', a, vf)

    out = lax.map(block, (qb, starts))
    return out.transpose(1, 2, 0, 3, 4).reshape(B, H, S, D)


def differential_attention(q, k, v, lam):
    B, H, _, S, Dh = q.shape
    nb = S // Q_BLOCK
    scale = 1.0 / math.sqrt(Dh)
    qb = q.reshape(B, H, 2, nb, Q_BLOCK, Dh).transpose(3, 0, 1, 2, 4, 5)
    starts = jnp.arange(nb, dtype=jnp.int32) * Q_BLOCK
    key_pos = jnp.arange(S, dtype=jnp.int32)
    kf = k.astype(jnp.float32)
    vf = v.astype(jnp.float32)

    def block(args):
        qblk, t0 = args
        s = jnp.einsum('bhcqd,bhckd->bhcqk', qblk.astype(jnp.float32), kf) * scale
        tq = t0 + jnp.arange(Q_BLOCK, dtype=jnp.int32)
        causal = key_pos[None, :] <= tq[:, None]
        p = jax.nn.softmax(jnp.where(causal, s, -jnp.inf), axis=-1)
        a = p[:, :, 0] - lam * p[:, :, 1]
        return jnp.einsum('bhqk,bhke->bhqe', a, vf)

    out = lax.map(block, (qb, starts))
    return out.transpose(1, 2, 0, 3, 4).reshape(B, H, S, 2 * Dh)


def stick_breaking_layer(x, norm_g, w_in, w_out):
    B, S, _ = x.shape
    h = rms_norm(x, norm_g)
    q, k, v, gate = jnp.split(h @ w_in, 4, axis=-1)

    def heads(t):
        return t.reshape(B, S, SB_HEADS, SB_HEAD_DIM).transpose(0, 2, 1, 3)

    o = stick_breaking_attention(heads(q), heads(k), heads(v))
    o = o.transpose(0, 2, 1, 3).reshape(B, S, D_MODEL).astype(x.dtype)
    return (jax.nn.silu(gate) * o) @ w_out


def shared_kv(x, kv_norm_g, w_kv, k_norm_g, cos, sin):
    B, S, _ = x.shape
    h = rms_norm(x, kv_norm_g)
    k, v = jnp.split(h @ w_kv, 2, axis=-1)
    k = k.reshape(B, S, DIFF_HEADS, 2, DIFF_HEAD_DIM)
    k = apply_rope(rms_norm(k, k_norm_g), cos[:, :, None, None, :], sin[:, :, None, None, :])
    k = k.transpose(0, 2, 3, 1, 4)
    v = v.reshape(B, S, DIFF_HEADS, DIFF_V_DIM).transpose(0, 2, 1, 3)
    return k, v


def diff_layer(x, k, v, norm_g, w_in, q_norm_g, lam_params, subln_g, w_out, layer_idx, cos, sin):
    B, S, _ = x.shape
    h = rms_norm(x, norm_g)
    q, gate = jnp.split(h @ w_in, 2, axis=-1)
    q = q.reshape(B, S, DIFF_HEADS, 2, DIFF_HEAD_DIM)
    q = apply_rope(rms_norm(q, q_norm_g), cos[:, :, None, None, :], sin[:, :, None, None, :])
    q = q.transpose(0, 2, 3, 1, 4)
    lam_init = 0.8 - 0.6 * math.exp(-0.3 * layer_idx)
    lf = lam_params.astype(jnp.float32)
    lam = jnp.exp(jnp.sum(lf[0] * lf[1])) - jnp.exp(jnp.sum(lf[2] * lf[3])) + lam_init
    o = differential_attention(q, k, v, lam)
    o = rms_norm(o, subln_g) * (1.0 - lam_init)
    o = o.transpose(0, 2, 1, 3).reshape(B, S, D_MODEL).astype(x.dtype)
    return (jax.nn.silu(gate) * o) @ w_out


def setup_inputs(seed: int = 0) -> dict:
    key = jax.random.key(seed)
    ks = jax.random.split(key, 16)
    d = D_MODEL
    s_in = d ** -0.5
    x = jax.random.normal(ks[0], (BATCH, SEQ, d), jnp.float32)
    positions = jnp.broadcast_to(jnp.arange(SEQ, dtype=jnp.int32)[None, :], (BATCH, SEQ))
    a_norm_g = 1.0 + 0.02 * jax.random.normal(ks[1], (N_A_LAYERS, d), jnp.float32)
    a_w_in = jax.random.normal(ks[2], (N_A_LAYERS, d, 4 * d), jnp.float32) * s_in
    a_w_out = jax.random.normal(ks[3], (N_A_LAYERS, d, d), jnp.float32) * s_in
    kv_norm_g = 1.0 + 0.02 * jax.random.normal(ks[4], (d,), jnp.float32)
    w_kv = jax.random.normal(ks[5], (d, 2 * d), jnp.float32) * s_in
    k_norm_g = 1.0 + 0.02 * jax.random.normal(ks[6], (DIFF_HEAD_DIM,), jnp.float32)
    b_norm_g = 1.0 + 0.02 * jax.random.normal(ks[7], (N_B_LAYERS, d), jnp.float32)
    b_w_in = jax.random.normal(ks[8], (N_B_LAYERS, d, 2 * d), jnp.float32) * s_in
    b_q_norm_g = 1.0 + 0.02 * jax.random.normal(ks[9], (N_B_LAYERS, DIFF_HEAD_DIM), jnp.float32)
    b_lambda = 0.1 * jax.random.normal(ks[10], (N_B_LAYERS, 4, DIFF_HEAD_DIM), jnp.float32)
    b_subln_g = 1.0 + 0.02 * jax.random.normal(ks[11], (N_B_LAYERS, DIFF_V_DIM), jnp.float32)
    b_w_out = jax.random.normal(ks[12], (N_B_LAYERS, d, d), jnp.float32) * s_in
    return {"x": x, "positions": positions, "a_norm_g": a_norm_g, "a_w_in": a_w_in,
            "a_w_out": a_w_out, "kv_norm_g": kv_norm_g, "w_kv": w_kv, "k_norm_g": k_norm_g,
            "b_norm_g": b_norm_g, "b_w_in": b_w_in, "b_q_norm_g": b_q_norm_g,
            "b_lambda": b_lambda, "b_subln_g": b_subln_g, "b_w_out": b_w_out}


def reference(x, positions, a_norm_g, a_w_in, a_w_out, kv_norm_g, w_kv, k_norm_g,
              b_norm_g, b_w_in, b_q_norm_g, b_lambda, b_subln_g, b_w_out):
    cos, sin = rope_tables(positions, DIFF_HEAD_DIM)
    k_shared = None
    v_shared = None
    for layer in range(DEPTH):
        if layer < N_A_LAYERS:
            x = x + stick_breaking_layer(x, a_norm_g[layer], a_w_in[layer], a_w_out[layer])
        else:
            if layer == N_A_LAYERS:
                k_shared, v_shared = shared_kv(x, kv_norm_g, w_kv, k_norm_g, cos, sin)
            j = layer - N_A_LAYERS
            x = x + diff_layer(x, k_shared, v_shared, b_norm_g[j], b_w_in[j], b_q_norm_g[j],
                               b_lambda[j], b_subln_g[j], b_w_out[j], layer, cos, sin)
    return x
```

```python
import functools
import math

import jax
import jax.numpy as jnp
from jax import lax
from jax.experimental import pallas as pl
from jax.experimental.pallas import tpu as pltpu

D_MODEL = 1024
HEAD_DIM = 64
LANES = 128
ROPE_THETA = 10000.0
EPS = 1e-6
DEPTH = 2
N_A_LAYERS = DEPTH // 2
VMEM_LIMIT_BYTES = 56 * 1024 * 1024

PROJ_ROWS = 512
PROJ_COLS = 1024
ATT_BLOCK = 256
NEG_BIG = -1e30

_NT = (((1,), (1,)), ((), ()))


def _params(*sem):
    return pltpu.CompilerParams(dimension_semantics=sem, vmem_limit_bytes=VMEM_LIMIT_BYTES)


def _rope_kernel(pos_ref, freq_ref, cos_ref, sin_ref):
    ang = pos_ref[...].astype(jnp.float32) * freq_ref[...]
    lane = lax.broadcasted_iota(jnp.int32, ang.shape, 1)
    first_half = (lane % HEAD_DIM) < (HEAD_DIM // 2)
    cos_ref[...] = jnp.cos(ang)
    sin = jnp.sin(ang)
    sin_ref[...] = jnp.where(first_half, -sin, sin)


def _rope_tables(positions):
    n = positions.size
    rows = 1024
    inv_freq = ROPE_THETA ** (-jnp.arange(0, HEAD_DIM, 2, dtype=jnp.float32) / HEAD_DIM)
    freq = jnp.tile(inv_freq, LANES // (HEAD_DIM // 2)).reshape(1, LANES)
    out = jax.ShapeDtypeStruct((n, LANES), jnp.float32)
    return pl.pallas_call(
        _rope_kernel,
        grid=(n // rows,),
        in_specs=[pl.BlockSpec((rows, 1), lambda i: (i, 0)),
                  pl.BlockSpec((1, LANES), lambda i: (0, 0))],
        out_specs=[pl.BlockSpec((rows, LANES), lambda i: (i, 0))] * 2,
        out_shape=[out, out],
        compiler_params=_params("parallel"),
        name="rope_tables",
    )(positions.reshape(n, 1), freq)


def _rms_scale(x):
    return x * lax.rsqrt(jnp.mean(x * x, axis=-1, keepdims=True) + EPS)


def _proj_a_kernel(x_ref, g_ref, w_ref, o_ref, h_ref):
    @pl.when(pl.program_id(1) == 0)
    def _():
        h_ref[...] = (_rms_scale(x_ref[...]) * g_ref[...]).astype(jnp.bfloat16)

    o_ref[...] = jnp.dot(h_ref[...], w_ref[...],
                         preferred_element_type=jnp.float32).astype(o_ref.dtype)


def _proj_a(x2, g, w):
    m, d = x2.shape
    n = w.shape[1]
    return pl.pallas_call(
        _proj_a_kernel,
        grid=(m // PROJ_ROWS, n // PROJ_COLS),
        in_specs=[pl.BlockSpec((PROJ_ROWS, d), lambda i, j: (i, 0)),
                  pl.BlockSpec((1, d), lambda i, j: (0, 0)),
                  pl.BlockSpec((d, PROJ_COLS), lambda i, j: (0, j))],
        out_specs=pl.BlockSpec((PROJ_ROWS, PROJ_COLS), lambda i, j: (i, j)),
        out_shape=jax.ShapeDtypeStruct((m, n), jnp.bfloat16),
        scratch_shapes=[pltpu.VMEM((PROJ_ROWS, d), jnp.bfloat16)],
        compiler_params=_params("parallel", "arbitrary"),
        name="proj_a",
    )(x2, g.reshape(1, d), w)


def _split_bf16(x):
    hi = x.astype(jnp.bfloat16)
    lo = (x - hi.astype(jnp.float32)).astype(jnp.bfloat16)
    return hi, lo


def _head_norm_rope(acc, ng, cos, sin, seg_mean):
    width = seg_mean.shape[0]
    lane = lax.broadcasted_iota(jnp.int32, cos.shape, 1)
    first_half = (lane % HEAD_DIM) < (HEAD_DIM // 2)
    outs = []
    for c in range(acc.shape[1] // width):
        t = acc[:, c * width:(c + 1) * width]
        hi, lo = _split_bf16(t * t)
        ms = (jnp.dot(hi, seg_mean, preferred_element_type=jnp.float32)
              + jnp.dot(lo, seg_mean, preferred_element_type=jnp.float32))
        tn = t * lax.rsqrt(ms + EPS) * ng[:, c * width:(c + 1) * width]
        for s in range(width // LANES):
            u = tn[:, s * LANES:(s + 1) * LANES]
            swapped = jnp.where(first_half,
                                pltpu.roll(u, LANES - HEAD_DIM // 2, axis=1),
                                pltpu.roll(u, HEAD_DIM // 2, axis=1))
            outs.append(u * cos + swapped * sin)
    return jnp.concatenate(outs, axis=1)


def _proj_b_kernel(x_ref, g_ref, w_ref, ng_ref, cos_ref, sin_ref, seg_ref, o_ref, h_ref):
    j = pl.program_id(1)

    @pl.when(j == 0)
    def _():
        xn = _rms_scale(x_ref[...])
        h_ref[0] = (xn * g_ref[0]).astype(jnp.bfloat16)
        h_ref[1] = (xn * g_ref[1]).astype(jnp.bfloat16)

    acc = jnp.dot(h_ref[j // 2], w_ref[...], preferred_element_type=jnp.float32)

    @pl.when(j % 2 == 0)
    def _():
        o_ref[...] = _head_norm_rope(acc, ng_ref[0], cos_ref[...], sin_ref[...],
                                     seg_ref[...]).astype(o_ref.dtype)

    @pl.when(j % 2 == 1)
    def _():
        o_ref[...] = acc.astype(o_ref.dtype)


def _proj_b(x2, g2, w, ng2, cos, sin):
    m, d = x2.shape
    n = w.shape[1]
    width = 2 * LANES
    seg = (jnp.arange(width)[:, None] // HEAD_DIM == jnp.arange(width)[None, :] // HEAD_DIM)
    seg_mean = (seg.astype(jnp.float32) / HEAD_DIM).astype(jnp.bfloat16)
    return pl.pallas_call(
        _proj_b_kernel,
        grid=(m // PROJ_ROWS, n // PROJ_COLS),
        in_specs=[pl.BlockSpec((PROJ_ROWS, d), lambda i, j: (i, 0)),
                  pl.BlockSpec((2, 1, d), lambda i, j: (0, 0, 0)),
                  pl.BlockSpec((d, PROJ_COLS), lambda i, j: (0, j)),
                  pl.BlockSpec((1, 1, PROJ_COLS), lambda i, j: (j // 2, 0, 0)),
                  pl.BlockSpec((PROJ_ROWS, LANES), lambda i, j: (i, 0)),
                  pl.BlockSpec((PROJ_ROWS, LANES), lambda i, j: (i, 0)),
                  pl.BlockSpec((width, width), lambda i, j: (0, 0))],
        out_specs=pl.BlockSpec((PROJ_ROWS, PROJ_COLS), lambda i, j: (i, j)),
        out_shape=jax.ShapeDtypeStruct((m, n), jnp.bfloat16),
        scratch_shapes=[pltpu.VMEM((2, PROJ_ROWS, d), jnp.bfloat16)],
        compiler_params=_params("parallel", "arbitrary"),
        name="proj_b",
    )(x2, g2, w, ng2, cos, sin, seg_mean)


def _out_proj_kernel(gate_ref, o_ref, x_ref, w_ref, y_ref):
    gate = gate_ref[...].astype(jnp.float32)
    u = (gate * jax.nn.sigmoid(gate) * o_ref[...].astype(jnp.float32)).astype(jnp.bfloat16)
    y_ref[...] = x_ref[...] + jnp.dot(u, w_ref[...], preferred_element_type=jnp.float32)


def _out_proj(proj, gate_col_block, o, x2, w):
    m, d = x2.shape
    return pl.pallas_call(
        _out_proj_kernel,
        grid=(m // PROJ_ROWS,),
        in_specs=[pl.BlockSpec((PROJ_ROWS, d), lambda i: (i, gate_col_block)),
                  pl.BlockSpec((PROJ_ROWS, d), lambda i: (i, 0)),
                  pl.BlockSpec((PROJ_ROWS, d), lambda i: (i, 0)),
                  pl.BlockSpec((d, d), lambda i: (0, 0))],
        out_specs=pl.BlockSpec((PROJ_ROWS, d), lambda i: (i, 0)),
        out_shape=jax.ShapeDtypeStruct((m, d), jnp.float32),
        compiler_params=_params("parallel"),
        name="out_proj",
    )(proj, o, x2, w)


def _sb_kernel(q_ref, k_ref, v_ref, tri_ref, o_ref, acc_ref, c_ref):
    tq = ATT_BLOCK
    i = pl.program_id(2)
    lane = lax.broadcasted_iota(jnp.int32, (1, LANES), 1)
    qn = q_ref[0] * jnp.asarray(-1.0 / math.sqrt(HEAD_DIM), jnp.bfloat16)
    qh = [jnp.where((lane // HEAD_DIM) == h, qn, jnp.zeros_like(qn)) for h in range(2)]
    tri = tri_ref[...]
    acc_ref[...] = jnp.zeros_like(acc_ref)
    c_ref[...] = jnp.zeros_like(c_ref)

    def block(j, diagonal):
        start = pl.multiple_of(j * tq, tq)
        kb = k_ref[0, pl.ds(start, tq), :]
        vb = v_ref[0, pl.ds(start, tq), :]
        if diagonal:
            row = lax.broadcasted_iota(jnp.int32, (tq, tq), 0)
            col = lax.broadcasted_iota(jnp.int32, (tq, tq), 1)
            strict = col < row
        for h in range(2):
            nz = lax.dot_general(qh[h], kb, _NT, preferred_element_type=jnp.float32)
            l1m = jnp.minimum(nz, 0.0) - jnp.log(1.0 + jnp.exp(-jnp.abs(nz)))
            if diagonal:
                l1m = jnp.where(strict, l1m, 0.0)
            hi, lo = _split_bf16(l1m)
            suffix = (jnp.dot(hi, tri, preferred_element_type=jnp.float32)
                      + jnp.dot(lo, tri, preferred_element_type=jnp.float32))
            carry = c_ref[h]
            a = jnp.exp((l1m - nz) + suffix + carry)
            if diagonal:
                a = jnp.where(strict, a, 0.0)
            acc_ref[h] += jnp.dot(a.astype(jnp.bfloat16), vb,
                                  preferred_element_type=jnp.float32)
            c_ref[h] = carry + suffix[:, :1] + l1m[:, :1]

    block(i, True)

    def body(t, _):
        block(i - 1 - t, False)
        return 0

    lax.fori_loop(0, i, body, 0)
    o_ref[0] = jnp.where((lane // HEAD_DIM) == 0, acc_ref[0], acc_ref[1]).astype(o_ref.dtype)


def _stick_breaking(qkvg, batch, seq):
    tq = ATT_BLOCK
    blocks = D_MODEL // LANES
    tri = (jnp.arange(tq)[:, None] > jnp.arange(tq)[None, :]).astype(jnp.bfloat16)
    return pl.pallas_call(
        _sb_kernel,
        grid=(batch, blocks, seq // tq),
        in_specs=[pl.BlockSpec((1, tq, LANES), lambda b, h, i: (b, i, h)),
                  pl.BlockSpec((1, seq, LANES), lambda b, h, i: (b, 0, blocks + h)),
                  pl.BlockSpec((1, seq, LANES), lambda b, h, i: (b, 0, 2 * blocks + h)),
                  pl.BlockSpec((tq, tq), lambda b, h, i: (0, 0))],
        out_specs=pl.BlockSpec((1, tq, LANES), lambda b, h, i: (b, i, h)),
        out_shape=jax.ShapeDtypeStruct((batch, seq, D_MODEL), jnp.bfloat16),
        scratch_shapes=[pltpu.VMEM((2, tq, LANES), jnp.float32),
                        pltpu.VMEM((2, tq, 1), jnp.float32)],
        compiler_params=_params("parallel", "parallel", "arbitrary"),
        name="stick_breaking",
    )(qkvg, qkvg, qkvg, tri)


def _diff_kernel(q_ref, k_ref, v_ref, lam_ref, sg_ref, o_ref, acc_ref, m_ref, l_ref, *,
                 lam_init):
    tq = ATT_BLOCK
    i = pl.program_id(2)
    lane = lax.broadcasted_iota(jnp.int32, (1, LANES), 1)
    qs = q_ref[0] * jnp.asarray(1.0 / math.sqrt(HEAD_DIM), jnp.bfloat16)
    qc = [jnp.where((lane // HEAD_DIM) == c, qs, jnp.zeros_like(qs)) for c in range(2)]

    def block(j, diagonal):
        start = pl.multiple_of(j * tq, tq)
        kb = k_ref[0, pl.ds(start, tq), :]
        vb = v_ref[0, pl.ds(start, tq), :]
        for c in range(2):
            s = lax.dot_general(qc[c], kb, _NT, preferred_element_type=jnp.float32)
            if diagonal:
                row = lax.broadcasted_iota(jnp.int32, (tq, tq), 0)
                col = lax.broadcasted_iota(jnp.int32, (tq, tq), 1)
                s = jnp.where(col <= row, s, NEG_BIG)
                m_new = jnp.max(s, axis=-1, keepdims=True)
                p = jnp.exp(s - m_new)
                l_ref[c] = jnp.sum(p, axis=-1, keepdims=True)
                acc_ref[c] = jnp.dot(p.astype(jnp.bfloat16), vb,
                                     preferred_element_type=jnp.float32)
            else:
                m_old = m_ref[c]
                m_new = jnp.maximum(m_old, jnp.max(s, axis=-1, keepdims=True))
                alpha = jnp.exp(m_old - m_new)
                p = jnp.exp(s - m_new)
                l_ref[c] = alpha * l_ref[c] + jnp.sum(p, axis=-1, keepdims=True)
                acc_ref[c] = alpha * acc_ref[c] + jnp.dot(p.astype(jnp.bfloat16), vb,
                                                          preferred_element_type=jnp.float32)
            m_ref[c] = m_new

    block(i, True)

    def body(t, _):
        block(i - 1 - t, False)
        return 0

    lax.fori_loop(0, i, body, 0)

    lp = lam_ref[...]
    lam = (jnp.exp(jnp.sum(lp[0:1] * lp[1:2], axis=-1, keepdims=True))
           - jnp.exp(jnp.sum(lp[2:3] * lp[3:4], axis=-1, keepdims=True)) + lam_init)
    o = acc_ref[0] / l_ref[0] - lam * (acc_ref[1] / l_ref[1])
    o = _rms_scale(o) * sg_ref[...] * (1.0 - lam_init)
    o_ref[0] = o.astype(o_ref.dtype)


def _diff_attention(proj, lam_params, subln_g, lam_init, batch, seq):
    tq = ATT_BLOCK
    heads = D_MODEL // LANES
    return pl.pallas_call(
        functools.partial(_diff_kernel, lam_init=lam_init),
        grid=(batch, heads, seq // tq),
        in_specs=[pl.BlockSpec((1, tq, LANES), lambda b, h, i: (b, i, 2 * heads + h)),
                  pl.BlockSpec((1, seq, LANES), lambda b, h, i: (b, 0, h)),
                  pl.BlockSpec((1, seq, LANES), lambda b, h, i: (b, 0, heads + h)),
                  pl.BlockSpec((4, HEAD_DIM), lambda b, h, i: (0, 0)),
                  pl.BlockSpec((1, LANES), lambda b, h, i: (0, 0))],
        out_specs=pl.BlockSpec((1, tq, LANES), lambda b, h, i: (b, i, h)),
        out_shape=jax.ShapeDtypeStruct((batch, seq, D_MODEL), jnp.bfloat16),
        scratch_shapes=[pltpu.VMEM((2, tq, LANES), jnp.float32),
                        pltpu.VMEM((2, tq, 1), jnp.float32),
                        pltpu.VMEM((2, tq, 1), jnp.float32)],
        compiler_params=_params("parallel", "parallel", "arbitrary"),
        name="diff_attention",
    )(proj, proj, proj, lam_params, subln_g.reshape(1, LANES))


def kernel(x, positions, a_norm_g, a_w_in, a_w_out, kv_norm_g, w_kv, k_norm_g, b_norm_g,
           b_w_in, b_q_norm_g, b_lambda, b_subln_g, b_w_out):
    batch, seq, d = x.shape
    assert d == D_MODEL and seq % ATT_BLOCK == 0 and (batch * seq) % PROJ_ROWS == 0
    assert a_norm_g.shape[0] == 1 and b_norm_g.shape[0] == 1
    bf16 = jnp.bfloat16
    x2 = x.reshape(batch * seq, d)

    qkvg = _proj_a(x2, a_norm_g[0], a_w_in[0].astype(bf16))
    o = _stick_breaking(qkvg.reshape(batch, seq, 4 * d), batch, seq)
    x2 = _out_proj(qkvg, 3, o.reshape(batch * seq, d), x2, a_w_out[0].astype(bf16))

    cos, sin = _rope_tables(positions)
    w_b = jnp.concatenate([w_kv, b_w_in[0]], axis=1).astype(bf16)
    g2 = jnp.stack([kv_norm_g, b_norm_g[0]]).reshape(2, 1, d)
    ng2 = jnp.stack([jnp.tile(k_norm_g, d // HEAD_DIM),
                     jnp.tile(b_q_norm_g[0], d // HEAD_DIM)]).reshape(2, 1, d)
    proj = _proj_b(x2, g2, w_b, ng2, cos, sin)
    layer_idx = N_A_LAYERS
    lam_init = 0.8 - 0.6 * math.exp(-0.3 * layer_idx)
    o = _diff_attention(proj.reshape(batch, seq, 4 * d), b_lambda[0], b_subln_g[0], lam_init,
                        batch, seq)
    x2 = _out_proj(proj, 3, o.reshape(batch * seq, d), x2, b_w_out[0].astype(bf16))
    return x2.reshape(batch, seq, d)
```

```python
import functools
import math

import jax
import jax.numpy as jnp
from jax import lax
from jax.experimental import pallas as pl
from jax.experimental.pallas import tpu as pltpu

D_MODEL = 1024
HEAD_DIM = 64
LANES = 128
ROPE_THETA = 10000.0
EPS = 1e-6
DEPTH = 2
N_A_LAYERS = DEPTH // 2
VMEM_LIMIT_BYTES = 56 * 1024 * 1024

PROJ_ROWS = 512
PROJ_COLS = 1024
ATT_BLOCK = 256
DIFF_BLOCK = 512
NEG_BIG = -1e30
LOG2E = math.log2(math.e)
UNDERFLOW_LOG2 = -150.0

_NT = (((1,), (1,)), ((), ()))


def _params(*sem):
    return pltpu.CompilerParams(dimension_semantics=sem, vmem_limit_bytes=VMEM_LIMIT_BYTES)


def _rope_kernel(pos_ref, freq_ref, cos_ref, sin_ref):
    ang = pos_ref[...].astype(jnp.float32) * freq_ref[...]
    lane = lax.broadcasted_iota(jnp.int32, ang.shape, 1)
    first_half = (lane % HEAD_DIM) < (HEAD_DIM // 2)
    cos_ref[...] = jnp.cos(ang)
    sin = jnp.sin(ang)
    sin_ref[...] = jnp.where(first_half, -sin, sin)


def _rope_tables(positions):
    n = positions.size
    rows = 1024
    inv_freq = ROPE_THETA ** (-jnp.arange(0, HEAD_DIM, 2, dtype=jnp.float32) / HEAD_DIM)
    freq = jnp.tile(inv_freq, LANES // (HEAD_DIM // 2)).reshape(1, LANES)
    out = jax.ShapeDtypeStruct((n, LANES), jnp.float32)
    return pl.pallas_call(
        _rope_kernel,
        grid=(n // rows,),
        in_specs=[pl.BlockSpec((rows, 1), lambda i: (i, 0)),
                  pl.BlockSpec((1, LANES), lambda i: (0, 0))],
        out_specs=[pl.BlockSpec((rows, LANES), lambda i: (i, 0))] * 2,
        out_shape=[out, out],
        compiler_params=_params("parallel"),
        name="rope_tables",
    )(positions.reshape(n, 1), freq)


def _rms_scale(x):
    return x * lax.rsqrt(jnp.mean(x * x, axis=-1, keepdims=True) + EPS)


def _proj_a_kernel(x_ref, g_ref, w_ref, o_ref, h_ref):
    @pl.when(pl.program_id(1) == 0)
    def _():
        h_ref[...] = (_rms_scale(x_ref[...]) * g_ref[...]).astype(jnp.bfloat16)

    o_ref[...] = jnp.dot(h_ref[...], w_ref[...],
                         preferred_element_type=jnp.float32).astype(o_ref.dtype)


def _proj_a(x2, g, w):
    m, d = x2.shape
    n = w.shape[1]
    return pl.pallas_call(
        _proj_a_kernel,
        grid=(m // PROJ_ROWS, n // PROJ_COLS),
        in_specs=[pl.BlockSpec((PROJ_ROWS, d), lambda i, j: (i, 0)),
                  pl.BlockSpec((1, d), lambda i, j: (0, 0)),
                  pl.BlockSpec((d, PROJ_COLS), lambda i, j: (0, j))],
        out_specs=pl.BlockSpec((PROJ_ROWS, PROJ_COLS), lambda i, j: (i, j)),
        out_shape=jax.ShapeDtypeStruct((m, n), jnp.bfloat16),
        scratch_shapes=[pltpu.VMEM((PROJ_ROWS, d), jnp.bfloat16)],
        compiler_params=_params("parallel", "arbitrary"),
        name="proj_a",
    )(x2, g.reshape(1, d), w)


def _split_bf16(x):
    hi = x.astype(jnp.bfloat16)
    lo = (x - hi.astype(jnp.float32)).astype(jnp.bfloat16)
    return hi, lo


def _head_norm_rope(acc, ng, cos, sin, seg_mean):
    width = seg_mean.shape[0]
    lane = lax.broadcasted_iota(jnp.int32, cos.shape, 1)
    first_half = (lane % HEAD_DIM) < (HEAD_DIM // 2)
    outs = []
    for c in range(acc.shape[1] // width):
        t = acc[:, c * width:(c + 1) * width]
        hi, lo = _split_bf16(t * t)
        ms = (jnp.dot(hi, seg_mean, preferred_element_type=jnp.float32)
              + jnp.dot(lo, seg_mean, preferred_element_type=jnp.float32))
        tn = t * lax.rsqrt(ms + EPS) * ng[:, c * width:(c + 1) * width]
        for s in range(width // LANES):
            u = tn[:, s * LANES:(s + 1) * LANES]
            swapped = jnp.where(first_half,
                                pltpu.roll(u, LANES - HEAD_DIM // 2, axis=1),
                                pltpu.roll(u, HEAD_DIM // 2, axis=1))
            outs.append(u * cos + swapped * sin)
    return jnp.concatenate(outs, axis=1)


def _proj_b_kernel(x_ref, g_ref, w_ref, ng_ref, cos_ref, sin_ref, seg_ref, o_ref, h_ref):
    j = pl.program_id(1)

    @pl.when(j == 0)
    def _():
        xn = _rms_scale(x_ref[...])
        h_ref[0] = (xn * g_ref[0]).astype(jnp.bfloat16)
        h_ref[1] = (xn * g_ref[1]).astype(jnp.bfloat16)

    acc = jnp.dot(h_ref[j // 2], w_ref[...], preferred_element_type=jnp.float32)

    @pl.when(j % 2 == 0)
    def _():
        o_ref[...] = _head_norm_rope(acc, ng_ref[0], cos_ref[...], sin_ref[...],
                                     seg_ref[...]).astype(o_ref.dtype)

    @pl.when(j % 2 == 1)
    def _():
        o_ref[...] = acc.astype(o_ref.dtype)


def _proj_b(x2, g2, w, ng2, cos, sin):
    m, d = x2.shape
    n = w.shape[1]
    width = 2 * LANES
    seg = (jnp.arange(width)[:, None] // HEAD_DIM == jnp.arange(width)[None, :] // HEAD_DIM)
    seg_mean = (seg.astype(jnp.float32) / HEAD_DIM).astype(jnp.bfloat16)
    return pl.pallas_call(
        _proj_b_kernel,
        grid=(m // PROJ_ROWS, n // PROJ_COLS),
        in_specs=[pl.BlockSpec((PROJ_ROWS, d), lambda i, j: (i, 0)),
                  pl.BlockSpec((2, 1, d), lambda i, j: (0, 0, 0)),
                  pl.BlockSpec((d, PROJ_COLS), lambda i, j: (0, j)),
                  pl.BlockSpec((1, 1, PROJ_COLS), lambda i, j: (j // 2, 0, 0)),
                  pl.BlockSpec((PROJ_ROWS, LANES), lambda i, j: (i, 0)),
                  pl.BlockSpec((PROJ_ROWS, LANES), lambda i, j: (i, 0)),
                  pl.BlockSpec((width, width), lambda i, j: (0, 0))],
        out_specs=pl.BlockSpec((PROJ_ROWS, PROJ_COLS), lambda i, j: (i, j)),
        out_shape=jax.ShapeDtypeStruct((m, n), jnp.bfloat16),
        scratch_shapes=[pltpu.VMEM((2, PROJ_ROWS, d), jnp.bfloat16)],
        compiler_params=_params("parallel", "arbitrary"),
        name="proj_b",
    )(x2, g2, w, ng2, cos, sin, seg_mean)


def _out_proj_kernel(gate_ref, o_ref, x_ref, w_ref, y_ref):
    gate = gate_ref[...].astype(jnp.float32)
    u = (gate * jax.nn.sigmoid(gate) * o_ref[...].astype(jnp.float32)).astype(jnp.bfloat16)
    y_ref[...] = x_ref[...] + jnp.dot(u, w_ref[...], preferred_element_type=jnp.float32)


def _out_proj(proj, gate_col_block, o, x2, w):
    m, d = x2.shape
    return pl.pallas_call(
        _out_proj_kernel,
        grid=(m // PROJ_ROWS,),
        in_specs=[pl.BlockSpec((PROJ_ROWS, d), lambda i: (i, gate_col_block)),
                  pl.BlockSpec((PROJ_ROWS, d), lambda i: (i, 0)),
                  pl.BlockSpec((PROJ_ROWS, d), lambda i: (i, 0)),
                  pl.BlockSpec((d, d), lambda i: (0, 0))],
        out_specs=pl.BlockSpec((PROJ_ROWS, d), lambda i: (i, 0)),
        out_shape=jax.ShapeDtypeStruct((m, d), jnp.float32),
        compiler_params=_params("parallel"),
        name="out_proj",
    )(proj, o, x2, w)


def _sb_kernel(q_ref, k_ref, v_ref, tri_ref, o_ref, acc_ref, c_ref, nz_ref, a_ref):
    tq = ATT_BLOCK
    i = pl.program_id(2)
    heads = range(2)
    lane = lax.broadcasted_iota(jnp.int32, (1, LANES), 1)
    qn = q_ref[0] * jnp.asarray(-1.0 / math.sqrt(HEAD_DIM), jnp.bfloat16)
    qh = [jnp.where((lane // HEAD_DIM) == h, qn, jnp.zeros_like(qn)) for h in heads]
    tri2 = tri_ref[...]

    def tile(ref, j):
        return ref[0, pl.ds(pl.multiple_of(j * tq, tq), tq), :]

    def scores(j):
        kb = tile(k_ref, j)
        return [lax.dot_general(qh[h], kb, _NT, preferred_element_type=jnp.float32) * LOG2E
                for h in heads]

    def weights(n, carry, strict):
        l = jnp.minimum(n, 0.0) - jnp.log2(1.0 + jnp.exp2(-jnp.abs(n)))
        if strict is not None:
            l = jnp.where(strict, l, 0.0)
        hi, lo = _split_bf16(l)
        incl = jnp.dot(jnp.concatenate([hi, lo], axis=1), tri2,
                       preferred_element_type=jnp.float32)
        a = jnp.exp2(incl - n + carry)
        if strict is not None:
            a = jnp.where(strict, a, 0.0)
        return a.astype(jnp.bfloat16), carry + incl[:, :1]

    def step(j, strict, first):
        n_next = scores(jnp.maximum(j - 1, 0))
        n = [nz_ref[h] for h in heads]
        if not first:
            vb = tile(v_ref, j + 1)
            for h in heads:
                acc_ref[h] += jnp.dot(a_ref[h], vb, preferred_element_type=jnp.float32)
        carries = []
        for h in heads:
            carry = jnp.zeros((tq, 1), jnp.float32) if first else c_ref[h]
            a_ref[h], carry = weights(n[h], carry, strict)
            c_ref[h] = carry
            carries.append(carry)
            nz_ref[h] = n_next[h]
        return jnp.max(jnp.maximum(carries[0], carries[1]))

    n0 = scores(i)
    for h in heads:
        nz_ref[h] = n0[h]
    acc_ref[...] = jnp.zeros_like(acc_ref)
    row = lax.broadcasted_iota(jnp.int32, (tq, tq), 0)
    col = lax.broadcasted_iota(jnp.int32, (tq, tq), 1)
    stick = step(i, col < row, True)

    def cond(state):
        t, stick = state
        return jnp.logical_and(t < i, stick > UNDERFLOW_LOG2)

    def body(state):
        t, _ = state
        return t + 1, step(i - 1 - t, None, False)

    done, _ = lax.while_loop(cond, body, (jnp.int32(0), stick))
    vb = tile(v_ref, i - done)
    out = [acc_ref[h] + jnp.dot(a_ref[h], vb, preferred_element_type=jnp.float32)
           for h in heads]
    o_ref[0] = jnp.where((lane // HEAD_DIM) == 0, out[0], out[1]).astype(o_ref.dtype)


def _stick_breaking(qkvg, batch, seq):
    tq = ATT_BLOCK
    blocks = D_MODEL // LANES
    tri = (jnp.arange(tq)[:, None] >= jnp.arange(tq)[None, :]).astype(jnp.bfloat16)
    tri2 = jnp.concatenate([tri, tri], axis=0)
    return pl.pallas_call(
        _sb_kernel,
        grid=(batch, blocks, seq // tq),
        in_specs=[pl.BlockSpec((1, tq, LANES), lambda b, h, i: (b, i, h)),
                  pl.BlockSpec((1, seq, LANES), lambda b, h, i: (b, 0, blocks + h)),
                  pl.BlockSpec((1, seq, LANES), lambda b, h, i: (b, 0, 2 * blocks + h)),
                  pl.BlockSpec((2 * tq, tq), lambda b, h, i: (0, 0))],
        out_specs=pl.BlockSpec((1, tq, LANES), lambda b, h, i: (b, i, h)),
        out_shape=jax.ShapeDtypeStruct((batch, seq, D_MODEL), jnp.bfloat16),
        scratch_shapes=[pltpu.VMEM((2, tq, LANES), jnp.float32),
                        pltpu.VMEM((2, tq, 1), jnp.float32),
                        pltpu.VMEM((2, tq, tq), jnp.float32),
                        pltpu.VMEM((2, tq, tq), jnp.bfloat16)],
        compiler_params=_params("parallel", "parallel", "arbitrary"),
        name="stick_breaking",
    )(qkvg, qkvg, qkvg, tri2)


def _diff_kernel(q_ref, k_ref, v_ref, lam_ref, sg_ref, o_ref, acc_ref, m_ref, alpha_ref, s_ref,
                 p_ref, *, lam_init):
    tq = DIFF_BLOCK
    i = pl.program_id(2)
    comps = range(2)
    lane = lax.broadcasted_iota(jnp.int32, (1, LANES), 1)
    qs = q_ref[0] * jnp.asarray(1.0 / math.sqrt(HEAD_DIM), jnp.bfloat16)
    qc = [jnp.where((lane // HEAD_DIM) == c, qs, jnp.zeros_like(qs)) for c in comps]
    ones = jnp.ones((tq, LANES), jnp.bfloat16)

    def tile(ref, j):
        return ref[0, pl.ds(pl.multiple_of(j * tq, tq), tq), :]

    def scores(j):
        kb = tile(k_ref, j)
        return [lax.dot_general(qc[c], kb, _NT, preferred_element_type=jnp.float32)
                for c in comps]

    def accumulate(j):
        v1 = jnp.concatenate([tile(v_ref, j), ones], axis=1)
        alpha = [jnp.concatenate([alpha_ref[c]] * 2, axis=1) for c in comps]
        return [alpha[c] * acc_ref[c]
                + jnp.dot(p_ref[c], v1, preferred_element_type=jnp.float32) for c in comps]

    def step(j, causal):
        s_next = scores(jnp.maximum(j - 1, 0))
        if causal is None:
            acc = accumulate(j + 1)
            for c in comps:
                acc_ref[c] = acc[c]
        for c in comps:
            s = s_ref[c]
            if causal is not None:
                s = jnp.where(causal, s, NEG_BIG)
            m_old = m_ref[c]
            m_new = jnp.maximum(m_old, jnp.broadcast_to(jnp.max(s, axis=-1, keepdims=True),
                                                        m_old.shape))
            alpha_ref[c] = jnp.exp(m_old - m_new)
            p_ref[c] = jnp.exp(s - jnp.concatenate([m_new] * (tq // LANES), axis=1)
                               ).astype(jnp.bfloat16)
            m_ref[c] = m_new
            s_ref[c] = s_next[c]

    s0 = scores(i)
    for c in comps:
        s_ref[c] = s0[c]
    acc_ref[...] = jnp.zeros_like(acc_ref)
    m_ref[...] = jnp.full_like(m_ref, NEG_BIG)
    row = lax.broadcasted_iota(jnp.int32, (tq, tq), 0)
    col = lax.broadcasted_iota(jnp.int32, (tq, tq), 1)
    step(i, col <= row)

    def body(t, _):
        step(i - 1 - t, None)
        return 0

    lax.fori_loop(0, i, body, 0)
    acc = accumulate(0)

    lp = lam_ref[...]
    lam = (jnp.exp(jnp.sum(lp[0:1] * lp[1:2], axis=-1, keepdims=True))
           - jnp.exp(jnp.sum(lp[2:3] * lp[3:4], axis=-1, keepdims=True)) + lam_init)
    o = (acc[0][:, :LANES] / acc[0][:, LANES:] - lam * (acc[1][:, :LANES] / acc[1][:, LANES:]))
    o = _rms_scale(o) * sg_ref[...] * (1.0 - lam_init)
    o_ref[0] = o.astype(o_ref.dtype)


def _diff_attention(proj, lam_params, subln_g, lam_init, batch, seq):
    tq = DIFF_BLOCK
    heads = D_MODEL // LANES
    return pl.pallas_call(
        functools.partial(_diff_kernel, lam_init=lam_init),
        grid=(batch, heads, seq // tq),
        in_specs=[pl.BlockSpec((1, tq, LANES), lambda b, h, i: (b, i, 2 * heads + h)),
                  pl.BlockSpec((1, seq, LANES), lambda b, h, i: (b, 0, h)),
                  pl.BlockSpec((1, seq, LANES), lambda b, h, i: (b, 0, heads + h)),
                  pl.BlockSpec((4, HEAD_DIM), lambda b, h, i: (0, 0)),
                  pl.BlockSpec((1, LANES), lambda b, h, i: (0, 0))],
        out_specs=pl.BlockSpec((1, tq, LANES), lambda b, h, i: (b, i, h)),
        out_shape=jax.ShapeDtypeStruct((batch, seq, D_MODEL), jnp.bfloat16),
        scratch_shapes=[pltpu.VMEM((2, tq, 2 * LANES), jnp.float32),
                        pltpu.VMEM((2, tq, LANES), jnp.float32),
                        pltpu.VMEM((2, tq, LANES), jnp.float32),
                        pltpu.VMEM((2, tq, tq), jnp.float32),
                        pltpu.VMEM((2, tq, tq), jnp.bfloat16)],
        compiler_params=_params("parallel", "parallel", "arbitrary"),
        name="diff_attention",
    )(proj, proj, proj, lam_params, subln_g.reshape(1, LANES))


def kernel(x, positions, a_norm_g, a_w_in, a_w_out, kv_norm_g, w_kv, k_norm_g, b_norm_g,
           b_w_in, b_q_norm_g, b_lambda, b_subln_g, b_w_out):
    batch, seq, d = x.shape
    assert d == D_MODEL and seq % ATT_BLOCK == 0 and seq % DIFF_BLOCK == 0
    assert (batch * seq) % PROJ_ROWS == 0
    assert a_norm_g.shape[0] == 1 and b_norm_g.shape[0] == 1
    bf16 = jnp.bfloat16
    x2 = x.reshape(batch * seq, d)

    qkvg = _proj_a(x2, a_norm_g[0], a_w_in[0].astype(bf16))
    o = _stick_breaking(qkvg.reshape(batch, seq, 4 * d), batch, seq)
    x2 = _out_proj(qkvg, 3, o.reshape(batch * seq, d), x2, a_w_out[0].astype(bf16))

    cos, sin = _rope_tables(positions)
    w_b = jnp.concatenate([w_kv, b_w_in[0]], axis=1).astype(bf16)
    g2 = jnp.stack([kv_norm_g, b_norm_g[0]]).reshape(2, 1, d)
    ng2 = jnp.stack([jnp.tile(k_norm_g, d // HEAD_DIM),
                     jnp.tile(b_q_norm_g[0], d // HEAD_DIM)]).reshape(2, 1, d)
    proj = _proj_b(x2, g2, w_b, ng2, cos, sin)
    layer_idx = N_A_LAYERS
    lam_init = 0.8 - 0.6 * math.exp(-0.3 * layer_idx)
    o = _diff_attention(proj.reshape(batch, seq, 4 * d), b_lambda[0], b_subln_g[0], lam_init,
                        batch, seq)
    x2 = _out_proj(proj, 3, o.reshape(batch * seq, d), x2, b_w_out[0].astype(bf16))
    return x2.reshape(batch, seq, d)
```

```python
import functools
import math

import jax
import jax.numpy as jnp
from jax import lax
from jax.experimental import pallas as pl
from jax.experimental.pallas import tpu as pltpu

D_MODEL = 1024
HEAD_DIM = 64
LANES = 128
ROPE_THETA = 10000.0
EPS = 1e-6
DEPTH = 2
N_A_LAYERS = DEPTH // 2
VMEM_LIMIT_BYTES = 56 * 1024 * 1024

PROJ_ROWS = 512
PROJ_COLS = 1024
ATT_BLOCK = 256
DIFF_BLOCK = 512
NEG_BIG = -1e30
LOG2E = math.log2(math.e)
DIFF_Q_SCALE = LOG2E / math.sqrt(HEAD_DIM)
SB_Q_SCALE = -LOG2E / math.sqrt(HEAD_DIM)
UNDERFLOW_LOG2 = -150.0

_NT = (((1,), (1,)), ((), ()))


def _params(*sem):
    return pltpu.CompilerParams(dimension_semantics=sem, vmem_limit_bytes=VMEM_LIMIT_BYTES)


def _rope_kernel(pos_ref, freq_ref, cos_ref, sin_ref):
    ang = pos_ref[...].astype(jnp.float32) * freq_ref[...]
    lane = lax.broadcasted_iota(jnp.int32, ang.shape, 1)
    first_half = (lane % HEAD_DIM) < (HEAD_DIM // 2)
    cos_ref[...] = jnp.cos(ang)
    sin = jnp.sin(ang)
    sin_ref[...] = jnp.where(first_half, -sin, sin)


def _rope_tables(positions):
    n = positions.size
    rows = 1024
    inv_freq = ROPE_THETA ** (-jnp.arange(0, HEAD_DIM, 2, dtype=jnp.float32) / HEAD_DIM)
    freq = jnp.tile(inv_freq, LANES // (HEAD_DIM // 2)).reshape(1, LANES)
    out = jax.ShapeDtypeStruct((n, LANES), jnp.float32)
    return pl.pallas_call(
        _rope_kernel,
        grid=(n // rows,),
        in_specs=[pl.BlockSpec((rows, 1), lambda i: (i, 0)),
                  pl.BlockSpec((1, LANES), lambda i: (0, 0))],
        out_specs=[pl.BlockSpec((rows, LANES), lambda i: (i, 0))] * 2,
        out_shape=[out, out],
        compiler_params=_params("parallel"),
        name="rope_tables",
    )(positions.reshape(n, 1), freq)


def _rms_scale(x):
    return x * lax.rsqrt(jnp.mean(x * x, axis=-1, keepdims=True) + EPS)


def _proj_a_kernel(x_ref, g_ref, w_ref, o_ref):
    h = (_rms_scale(x_ref[...]) * g_ref[...]).astype(jnp.bfloat16)
    for c in range(o_ref.shape[1] // PROJ_COLS):
        cols = slice(c * PROJ_COLS, (c + 1) * PROJ_COLS)
        acc = jnp.dot(h, w_ref[:, cols], preferred_element_type=jnp.float32)
        if c == 0:
            acc = acc * SB_Q_SCALE
        o_ref[:, cols] = acc.astype(o_ref.dtype)


def _proj_a(x2, g, w):
    m, d = x2.shape
    n = w.shape[1]
    return pl.pallas_call(
        _proj_a_kernel,
        grid=(m // PROJ_ROWS,),
        in_specs=[pl.BlockSpec((PROJ_ROWS, d), lambda i: (i, 0)),
                  pl.BlockSpec((1, d), lambda i: (0, 0)),
                  pl.BlockSpec((d, n), lambda i: (0, 0))],
        out_specs=pl.BlockSpec((PROJ_ROWS, n), lambda i: (i, 0)),
        out_shape=jax.ShapeDtypeStruct((m, n), jnp.bfloat16),
        compiler_params=_params("parallel"),
        name="proj_a",
    )(x2, g.reshape(1, d), w)


def _split_bf16(x):
    hi = x.astype(jnp.bfloat16)
    lo = (x - hi.astype(jnp.float32)).astype(jnp.bfloat16)
    return hi, lo


def _head_norm_rope(acc, ng, cos, sin, seg_mean):
    width = seg_mean.shape[0]
    lane = lax.broadcasted_iota(jnp.int32, cos.shape, 1)
    first_half = (lane % HEAD_DIM) < (HEAD_DIM // 2)
    outs = []
    for c in range(acc.shape[1] // width):
        t = acc[:, c * width:(c + 1) * width]
        hi, lo = _split_bf16(t * t)
        ms = (jnp.dot(hi, seg_mean, preferred_element_type=jnp.float32)
              + jnp.dot(lo, seg_mean, preferred_element_type=jnp.float32))
        tn = t * lax.rsqrt(ms + EPS) * ng[:, c * width:(c + 1) * width]
        for s in range(width // LANES):
            u = tn[:, s * LANES:(s + 1) * LANES]
            swapped = jnp.where(first_half,
                                pltpu.roll(u, LANES - HEAD_DIM // 2, axis=1),
                                pltpu.roll(u, HEAD_DIM // 2, axis=1))
            outs.append(u * cos + swapped * sin)
    return jnp.concatenate(outs, axis=1)


def _proj_b_kernel(x_ref, g_ref, w_ref, ng_ref, cos_ref, sin_ref, seg_ref, o_ref):
    xn = _rms_scale(x_ref[...])
    for c in range(o_ref.shape[1] // PROJ_COLS):
        cols = slice(c * PROJ_COLS, (c + 1) * PROJ_COLS)
        h = (xn * g_ref[c // 2]).astype(jnp.bfloat16)
        acc = jnp.dot(h, w_ref[:, cols], preferred_element_type=jnp.float32)
        if c % 2 == 0:
            acc = _head_norm_rope(acc, ng_ref[c // 2], cos_ref[...], sin_ref[...], seg_ref[...])
        if c == 2:
            acc = acc * DIFF_Q_SCALE
        o_ref[:, cols] = acc.astype(o_ref.dtype)


def _proj_b(x2, g2, w, ng2, cos, sin):
    m, d = x2.shape
    n = w.shape[1]
    width = 2 * LANES
    seg = (jnp.arange(width)[:, None] // HEAD_DIM == jnp.arange(width)[None, :] // HEAD_DIM)
    seg_mean = (seg.astype(jnp.float32) / HEAD_DIM).astype(jnp.bfloat16)
    return pl.pallas_call(
        _proj_b_kernel,
        grid=(m // PROJ_ROWS,),
        in_specs=[pl.BlockSpec((PROJ_ROWS, d), lambda i: (i, 0)),
                  pl.BlockSpec((2, 1, d), lambda i: (0, 0, 0)),
                  pl.BlockSpec((d, n), lambda i: (0, 0)),
                  pl.BlockSpec((2, 1, PROJ_COLS), lambda i: (0, 0, 0)),
                  pl.BlockSpec((PROJ_ROWS, LANES), lambda i: (i, 0)),
                  pl.BlockSpec((PROJ_ROWS, LANES), lambda i: (i, 0)),
                  pl.BlockSpec((width, width), lambda i: (0, 0))],
        out_specs=pl.BlockSpec((PROJ_ROWS, n), lambda i: (i, 0)),
        out_shape=jax.ShapeDtypeStruct((m, n), jnp.bfloat16),
        compiler_params=_params("parallel"),
        name="proj_b",
    )(x2, g2, w, ng2, cos, sin, seg_mean)


def _out_proj_kernel(gate_ref, o_ref, x_ref, w_ref, y_ref):
    gate = gate_ref[...].astype(jnp.float32)
    u = (gate * jax.nn.sigmoid(gate) * o_ref[...].astype(jnp.float32)).astype(jnp.bfloat16)
    y_ref[...] = x_ref[...] + jnp.dot(u, w_ref[...], preferred_element_type=jnp.float32)


def _out_proj(proj, gate_col_block, o, x2, w):
    m, d = x2.shape
    return pl.pallas_call(
        _out_proj_kernel,
        grid=(m // PROJ_ROWS,),
        in_specs=[pl.BlockSpec((PROJ_ROWS, d), lambda i: (i, gate_col_block)),
                  pl.BlockSpec((PROJ_ROWS, d), lambda i: (i, 0)),
                  pl.BlockSpec((PROJ_ROWS, d), lambda i: (i, 0)),
                  pl.BlockSpec((d, d), lambda i: (0, 0))],
        out_specs=pl.BlockSpec((PROJ_ROWS, d), lambda i: (i, 0)),
        out_shape=jax.ShapeDtypeStruct((m, d), jnp.float32),
        compiler_params=_params("parallel"),
        name="out_proj",
    )(proj, o, x2, w)


def _sb_kernel(q_ref, k_ref, v_ref, tri_ref, o_ref, acc_ref, c_ref, nz_ref, a_ref):
    tq = ATT_BLOCK
    i = pl.program_id(2)
    heads = range(2)
    lane = lax.broadcasted_iota(jnp.int32, (1, LANES), 1)
    qn = q_ref[0]
    qh = [jnp.where((lane // HEAD_DIM) == h, qn, jnp.zeros_like(qn)) for h in heads]
    tri2 = tri_ref[...]

    def tile(ref, j):
        return ref[0, pl.ds(pl.multiple_of(j * tq, tq), tq), :]

    def scores(j):
        kb = tile(k_ref, j)
        return [lax.dot_general(qh[h], kb, _NT, preferred_element_type=jnp.float32)
                for h in heads]

    def weights(n, carry, strict):
        l = jnp.minimum(n, 0.0) - jnp.log2(1.0 + jnp.exp2(-jnp.abs(n)))
        if strict is not None:
            l = jnp.where(strict, l, 0.0)
        hi, lo = _split_bf16(l)
        incl = jnp.dot(jnp.concatenate([hi, lo], axis=1), tri2,
                       preferred_element_type=jnp.float32)
        a = jnp.exp2(incl - n + carry)
        if strict is not None:
            a = jnp.where(strict, a, 0.0)
        return a.astype(jnp.bfloat16), carry + incl[:, :1]

    def step(j, strict, first):
        n_next = scores(jnp.maximum(j - 1, 0))
        n = [nz_ref[h] for h in heads]
        if not first:
            vb = tile(v_ref, j + 1)
            for h in heads:
                acc_ref[h] += jnp.dot(a_ref[h], vb, preferred_element_type=jnp.float32)
        carries = []
        for h in heads:
            carry = jnp.zeros((tq, 1), jnp.float32) if first else c_ref[h]
            a_ref[h], carry = weights(n[h], carry, strict)
            c_ref[h] = carry
            carries.append(carry)
            nz_ref[h] = n_next[h]
        return jnp.max(jnp.maximum(carries[0], carries[1]))

    n0 = scores(i)
    for h in heads:
        nz_ref[h] = n0[h]
    acc_ref[...] = jnp.zeros_like(acc_ref)
    row = lax.broadcasted_iota(jnp.int32, (tq, tq), 0)
    col = lax.broadcasted_iota(jnp.int32, (tq, tq), 1)
    stick = step(i, col < row, True)

    def cond(state):
        t, stick = state
        return jnp.logical_and(t < i, stick > UNDERFLOW_LOG2)

    def body(state):
        t, _ = state
        return t + 1, step(i - 1 - t, None, False)

    done, _ = lax.while_loop(cond, body, (jnp.int32(0), stick))
    vb = tile(v_ref, i - done)
    out = [acc_ref[h] + jnp.dot(a_ref[h], vb, preferred_element_type=jnp.float32)
           for h in heads]
    o_ref[0] = jnp.where((lane // HEAD_DIM) == 0, out[0], out[1]).astype(o_ref.dtype)


def _stick_breaking(qkvg, batch, seq):
    tq = ATT_BLOCK
    blocks = D_MODEL // LANES
    tri = (jnp.arange(tq)[:, None] >= jnp.arange(tq)[None, :]).astype(jnp.bfloat16)
    tri2 = jnp.concatenate([tri, tri], axis=0)
    return pl.pallas_call(
        _sb_kernel,
        grid=(batch, blocks, seq // tq),
        in_specs=[pl.BlockSpec((1, tq, LANES), lambda b, h, i: (b, i, h)),
                  pl.BlockSpec((1, seq, LANES), lambda b, h, i: (b, 0, blocks + h)),
                  pl.BlockSpec((1, seq, LANES), lambda b, h, i: (b, 0, 2 * blocks + h)),
                  pl.BlockSpec((2 * tq, tq), lambda b, h, i: (0, 0))],
        out_specs=pl.BlockSpec((1, tq, LANES), lambda b, h, i: (b, i, h)),
        out_shape=jax.ShapeDtypeStruct((batch, seq, D_MODEL), jnp.bfloat16),
        scratch_shapes=[pltpu.VMEM((2, tq, LANES), jnp.float32),
                        pltpu.VMEM((2, tq, 1), jnp.float32),
                        pltpu.VMEM((2, tq, tq), jnp.float32),
                        pltpu.VMEM((2, tq, tq), jnp.bfloat16)],
        compiler_params=_params("parallel", "parallel", "arbitrary"),
        name="stick_breaking",
    )(qkvg, qkvg, qkvg, tri2)


def _diff_kernel(q_ref, k_ref, v_ref, lam_ref, sg_ref, o_ref, acc_ref, m_ref, alpha_ref, s_ref,
                 p_ref, *, lam_init):
    tq = DIFF_BLOCK
    i = pl.program_id(2)
    comps = range(2)
    lane = lax.broadcasted_iota(jnp.int32, (1, LANES), 1)
    qs = q_ref[0]
    qc = [jnp.where((lane // HEAD_DIM) == c, qs, jnp.zeros_like(qs)) for c in comps]
    ones = jnp.ones((tq, LANES), jnp.bfloat16)

    def tile(ref, j):
        return ref[0, pl.ds(pl.multiple_of(j * tq, tq), tq), :]

    def scores(j):
        kb = tile(k_ref, j)
        return [lax.dot_general(qc[c], kb, _NT, preferred_element_type=jnp.float32)
                for c in comps]

    def accumulate(acc, alpha, p, j):
        v1 = jnp.concatenate([tile(v_ref, j), ones], axis=1)
        return [jnp.concatenate([alpha[c]] * 2, axis=1) * acc[c]
                + jnp.dot(p[c], v1, preferred_element_type=jnp.float32) for c in comps]

    def softmax_tile(s, m, causal):
        p, alpha, m_out = [], [], []
        for c in comps:
            sc = s[c] if causal is None else jnp.where(causal, s[c], NEG_BIG)
            m_new = jnp.maximum(m[c], jnp.broadcast_to(jnp.max(sc, axis=-1, keepdims=True),
                                                       m[c].shape))
            alpha.append(jnp.exp2(m[c] - m_new))
            p.append(jnp.exp2(sc - jnp.concatenate([m_new] * (tq // LANES), axis=1)
                              ).astype(jnp.bfloat16))
            m_out.append(m_new)
        return p, alpha, m_out

    def load(ref):
        return [ref[c] for c in comps]

    def store(ref, vals):
        for c in comps:
            ref[c] = vals[c]

    def step(j):
        s_next = scores(jnp.maximum(j - 1, 0))
        acc = accumulate(load(acc_ref), load(alpha_ref), load(p_ref), j + 1)
        p, alpha, m = softmax_tile(load(s_ref), load(m_ref), None)
        store(acc_ref, acc), store(p_ref, p), store(alpha_ref, alpha), store(m_ref, m)
        store(s_ref, s_next)

    row = lax.broadcasted_iota(jnp.int32, (tq, tq), 0)
    col = lax.broadcasted_iota(jnp.int32, (tq, tq), 1)
    s_diag = scores(i)
    s_next = scores(jnp.maximum(i - 1, 0))
    m0 = [jnp.full((tq, LANES), NEG_BIG, jnp.float32) for c in comps]
    p, alpha, m = softmax_tile(s_diag, m0, col <= row)
    store(p_ref, p), store(alpha_ref, alpha), store(m_ref, m), store(s_ref, s_next)
    acc_ref[...] = jnp.zeros_like(acc_ref)

    def body(t, _):
        step(i - 1 - t)
        return 0

    lax.fori_loop(0, i, body, 0)
    acc = accumulate(load(acc_ref), load(alpha_ref), load(p_ref), 0)

    lp = lam_ref[...]
    lam = (jnp.exp(jnp.sum(lp[0:1] * lp[1:2], axis=-1, keepdims=True))
           - jnp.exp(jnp.sum(lp[2:3] * lp[3:4], axis=-1, keepdims=True)) + lam_init)
    o = (acc[0][:, :LANES] / acc[0][:, LANES:] - lam * (acc[1][:, :LANES] / acc[1][:, LANES:]))
    o = _rms_scale(o) * sg_ref[...] * (1.0 - lam_init)
    o_ref[0] = o.astype(o_ref.dtype)


def _diff_attention(proj, lam_params, subln_g, lam_init, batch, seq):
    tq = DIFF_BLOCK
    heads = D_MODEL // LANES
    return pl.pallas_call(
        functools.partial(_diff_kernel, lam_init=lam_init),
        grid=(batch, heads, seq // tq),
        in_specs=[pl.BlockSpec((1, tq, LANES), lambda b, h, i: (b, i, 2 * heads + h)),
                  pl.BlockSpec((1, seq, LANES), lambda b, h, i: (b, 0, h)),
                  pl.BlockSpec((1, seq, LANES), lambda b, h, i: (b, 0, heads + h)),
                  pl.BlockSpec((4, HEAD_DIM), lambda b, h, i: (0, 0)),
                  pl.BlockSpec((1, LANES), lambda b, h, i: (0, 0))],
        out_specs=pl.BlockSpec((1, tq, LANES), lambda b, h, i: (b, i, h)),
        out_shape=jax.ShapeDtypeStruct((batch, seq, D_MODEL), jnp.bfloat16),
        scratch_shapes=[pltpu.VMEM((2, tq, 2 * LANES), jnp.float32),
                        pltpu.VMEM((2, tq, LANES), jnp.float32),
                        pltpu.VMEM((2, tq, LANES), jnp.float32),
                        pltpu.VMEM((2, tq, tq), jnp.float32),
                        pltpu.VMEM((2, tq, tq), jnp.bfloat16)],
        compiler_params=_params("parallel", "parallel", "arbitrary"),
        name="diff_attention",
    )(proj, proj, proj, lam_params, subln_g.reshape(1, LANES))


def kernel(x, positions, a_norm_g, a_w_in, a_w_out, kv_norm_g, w_kv, k_norm_g, b_norm_g,
           b_w_in, b_q_norm_g, b_lambda, b_subln_g, b_w_out):
    batch, seq, d = x.shape
    assert d == D_MODEL and seq % ATT_BLOCK == 0 and seq % DIFF_BLOCK == 0
    assert (batch * seq) % PROJ_ROWS == 0
    assert a_norm_g.shape[0] == 1 and b_norm_g.shape[0] == 1
    bf16 = jnp.bfloat16
    x2 = x.reshape(batch * seq, d)

    qkvg = _proj_a(x2, a_norm_g[0], a_w_in[0].astype(bf16))
    o = _stick_breaking(qkvg.reshape(batch, seq, 4 * d), batch, seq)
    x2 = _out_proj(qkvg, 3, o.reshape(batch * seq, d), x2, a_w_out[0].astype(bf16))

    cos, sin = _rope_tables(positions)
    w_b = jnp.concatenate([w_kv, b_w_in[0]], axis=1).astype(bf16)
    g2 = jnp.stack([kv_norm_g, b_norm_g[0]]).reshape(2, 1, d)
    ng2 = jnp.stack([jnp.tile(k_norm_g, d // HEAD_DIM),
                     jnp.tile(b_q_norm_g[0], d // HEAD_DIM)]).reshape(2, 1, d)
    proj = _proj_b(x2, g2, w_b, ng2, cos, sin)
    layer_idx = N_A_LAYERS
    lam_init = 0.8 - 0.6 * math.exp(-0.3 * layer_idx)
    o = _diff_attention(proj.reshape(batch, seq, 4 * d), b_lambda[0], b_subln_g[0], lam_init,
                        batch, seq)
    x2 = _out_proj(proj, 3, o.reshape(batch * seq, d), x2, b_w_out[0].astype(bf16))
    return x2.reshape(batch, seq, d)
```

```python
import functools
import math

import jax
import jax.numpy as jnp
from jax import lax
from jax.experimental import pallas as pl
from jax.experimental.pallas import tpu as pltpu

D_MODEL = 1024
HEAD_DIM = 64
LANES = 128
ROPE_THETA = 10000.0
EPS = 1e-6
DEPTH = 2
N_A_LAYERS = DEPTH // 2
VMEM_LIMIT_BYTES = 56 * 1024 * 1024

PROJ_ROWS = 512
PROJ_COLS = 1024
ATT_BLOCK = 256
SB_HEADS = 4
DIFF_BLOCK = 512
DIFF_KEYS = 512
NEG_BIG = -1e30
LOG2E = math.log2(math.e)
DIFF_Q_SCALE = LOG2E / math.sqrt(HEAD_DIM)
SB_Q_SCALE = -LOG2E / math.sqrt(HEAD_DIM)
UNDERFLOW_LOG2 = -150.0

_NT = (((1,), (1,)), ((), ()))


def _params(*sem):
    return pltpu.CompilerParams(dimension_semantics=sem, vmem_limit_bytes=VMEM_LIMIT_BYTES)


def _rope_kernel(pos_ref, freq_ref, cos_ref, sin_ref):
    ang = pos_ref[...].astype(jnp.float32) * freq_ref[...]
    lane = lax.broadcasted_iota(jnp.int32, ang.shape, 1)
    first_half = (lane % HEAD_DIM) < (HEAD_DIM // 2)
    cos_ref[...] = jnp.cos(ang)
    sin = jnp.sin(ang)
    sin_ref[...] = jnp.where(first_half, -sin, sin)


def _rope_tables(positions):
    n = positions.size
    rows = 1024
    inv_freq = ROPE_THETA ** (-jnp.arange(0, HEAD_DIM, 2, dtype=jnp.float32) / HEAD_DIM)
    freq = jnp.tile(inv_freq, LANES // (HEAD_DIM // 2)).reshape(1, LANES)
    out = jax.ShapeDtypeStruct((n, LANES), jnp.float32)
    return pl.pallas_call(
        _rope_kernel,
        grid=(n // rows,),
        in_specs=[pl.BlockSpec((rows, 1), lambda i: (i, 0)),
                  pl.BlockSpec((1, LANES), lambda i: (0, 0))],
        out_specs=[pl.BlockSpec((rows, LANES), lambda i: (i, 0))] * 2,
        out_shape=[out, out],
        compiler_params=_params("parallel"),
        name="rope_tables",
    )(positions.reshape(n, 1), freq)


def _rms_scale(x):
    return x * lax.rsqrt(jnp.mean(x * x, axis=-1, keepdims=True) + EPS)


def _proj_a_kernel(x_ref, g_ref, w_ref, o_ref):
    h = (_rms_scale(x_ref[...]) * g_ref[...]).astype(jnp.bfloat16)
    for c in range(o_ref.shape[1] // PROJ_COLS):
        cols = slice(c * PROJ_COLS, (c + 1) * PROJ_COLS)
        acc = jnp.dot(h, w_ref[:, cols], preferred_element_type=jnp.float32)
        if c == 0:
            acc = acc * SB_Q_SCALE
        o_ref[:, cols] = acc.astype(o_ref.dtype)


def _proj_a(x2, g, w):
    m, d = x2.shape
    n = w.shape[1]
    return pl.pallas_call(
        _proj_a_kernel,
        grid=(m // PROJ_ROWS,),
        in_specs=[pl.BlockSpec((PROJ_ROWS, d), lambda i: (i, 0)),
                  pl.BlockSpec((1, d), lambda i: (0, 0)),
                  pl.BlockSpec((d, n), lambda i: (0, 0))],
        out_specs=pl.BlockSpec((PROJ_ROWS, n), lambda i: (i, 0)),
        out_shape=jax.ShapeDtypeStruct((m, n), jnp.bfloat16),
        compiler_params=_params("parallel"),
        name="proj_a",
    )(x2, g.reshape(1, d), w)


def _split_bf16(x):
    hi = x.astype(jnp.bfloat16)
    lo = (x - hi.astype(jnp.float32)).astype(jnp.bfloat16)
    return hi, lo


def _head_norm_rope(acc, ng, cos, sin, seg_mean):
    width = seg_mean.shape[0]
    lane = lax.broadcasted_iota(jnp.int32, cos.shape, 1)
    first_half = (lane % HEAD_DIM) < (HEAD_DIM // 2)
    outs = []
    for c in range(acc.shape[1] // width):
        t = acc[:, c * width:(c + 1) * width]
        hi, lo = _split_bf16(t * t)
        ms = (jnp.dot(hi, seg_mean, preferred_element_type=jnp.float32)
              + jnp.dot(lo, seg_mean, preferred_element_type=jnp.float32))
        tn = t * lax.rsqrt(ms + EPS) * ng[:, c * width:(c + 1) * width]
        for s in range(width // LANES):
            u = tn[:, s * LANES:(s + 1) * LANES]
            swapped = jnp.where(first_half,
                                pltpu.roll(u, LANES - HEAD_DIM // 2, axis=1),
                                pltpu.roll(u, HEAD_DIM // 2, axis=1))
            outs.append(u * cos + swapped * sin)
    return jnp.concatenate(outs, axis=1)


def _proj_b_kernel(x_ref, g_ref, w_ref, ng_ref, cos_ref, sin_ref, seg_ref, o_ref):
    xn = _rms_scale(x_ref[...])
    for c in range(o_ref.shape[1] // PROJ_COLS):
        cols = slice(c * PROJ_COLS, (c + 1) * PROJ_COLS)
        h = (xn * g_ref[c // 2]).astype(jnp.bfloat16)
        acc = jnp.dot(h, w_ref[:, cols], preferred_element_type=jnp.float32)
        if c % 2 == 0:
            acc = _head_norm_rope(acc, ng_ref[c // 2], cos_ref[...], sin_ref[...], seg_ref[...])
        if c == 2:
            acc = acc * DIFF_Q_SCALE
        o_ref[:, cols] = acc.astype(o_ref.dtype)


def _proj_b(x2, g2, w, ng2, cos, sin):
    m, d = x2.shape
    n = w.shape[1]
    width = 2 * LANES
    seg = (jnp.arange(width)[:, None] // HEAD_DIM == jnp.arange(width)[None, :] // HEAD_DIM)
    seg_mean = (seg.astype(jnp.float32) / HEAD_DIM).astype(jnp.bfloat16)
    return pl.pallas_call(
        _proj_b_kernel,
        grid=(m // PROJ_ROWS,),
        in_specs=[pl.BlockSpec((PROJ_ROWS, d), lambda i: (i, 0)),
                  pl.BlockSpec((2, 1, d), lambda i: (0, 0, 0)),
                  pl.BlockSpec((d, n), lambda i: (0, 0)),
                  pl.BlockSpec((2, 1, PROJ_COLS), lambda i: (0, 0, 0)),
                  pl.BlockSpec((PROJ_ROWS, LANES), lambda i: (i, 0)),
                  pl.BlockSpec((PROJ_ROWS, LANES), lambda i: (i, 0)),
                  pl.BlockSpec((width, width), lambda i: (0, 0))],
        out_specs=pl.BlockSpec((PROJ_ROWS, n), lambda i: (i, 0)),
        out_shape=jax.ShapeDtypeStruct((m, n), jnp.bfloat16),
        compiler_params=_params("parallel"),
        name="proj_b",
    )(x2, g2, w, ng2, cos, sin, seg_mean)


def _out_proj_kernel(gate_ref, o_ref, x_ref, w_ref, y_ref):
    gate = gate_ref[...].astype(jnp.float32)
    u = (gate * jax.nn.sigmoid(gate) * o_ref[...].astype(jnp.float32)).astype(jnp.bfloat16)
    y_ref[...] = x_ref[...] + jnp.dot(u, w_ref[...], preferred_element_type=jnp.float32)


def _out_proj(proj, gate_col_block, o, x2, w):
    m, d = x2.shape
    return pl.pallas_call(
        _out_proj_kernel,
        grid=(m // PROJ_ROWS,),
        in_specs=[pl.BlockSpec((PROJ_ROWS, d), lambda i: (i, gate_col_block)),
                  pl.BlockSpec((PROJ_ROWS, d), lambda i: (i, 0)),
                  pl.BlockSpec((PROJ_ROWS, d), lambda i: (i, 0)),
                  pl.BlockSpec((d, d), lambda i: (0, 0))],
        out_specs=pl.BlockSpec((PROJ_ROWS, d), lambda i: (i, 0)),
        out_shape=jax.ShapeDtypeStruct((m, d), jnp.float32),
        compiler_params=_params("parallel"),
        name="out_proj",
    )(proj, o, x2, w)


def _sb_kernel(q_ref, k_ref, v_ref, tri_ref, o_ref, acc_ref, c_ref, nz_ref, a_ref):
    tq = ATT_BLOCK
    i = pl.program_id(2)
    heads = range(SB_HEADS)
    lane = lax.broadcasted_iota(jnp.int32, (1, SB_HEADS * HEAD_DIM), 1)
    qn = q_ref[0]
    qh = [jnp.where((lane // HEAD_DIM) == h, qn, jnp.zeros_like(qn)) for h in heads]
    tri2 = tri_ref[...]

    def tile(ref, j):
        return ref[0, pl.ds(pl.multiple_of(j * tq, tq), tq), :]

    def scores(j):
        kb = tile(k_ref, j)
        return [lax.dot_general(qh[h], kb, _NT, preferred_element_type=jnp.float32)
                for h in heads]

    def weights(n, carry, strict):
        l = jnp.minimum(n, 0.0) - jnp.log2(1.0 + jnp.exp2(-jnp.abs(n)))
        if strict is not None:
            l = jnp.where(strict, l, 0.0)
        hi, lo = _split_bf16(l)
        incl = jnp.dot(jnp.concatenate([hi, lo], axis=1), tri2,
                       preferred_element_type=jnp.float32)
        a = jnp.exp2(incl - n + carry)
        if strict is not None:
            a = jnp.where(strict, a, 0.0)
        return a.astype(jnp.bfloat16), carry + incl[:, :1]

    def step(j, strict, first):
        n_next = scores(jnp.maximum(j - 1, 0))
        n = [nz_ref[h] for h in heads]
        if not first:
            vb = tile(v_ref, j + 1)
            for h in heads:
                acc_ref[h] += jnp.dot(a_ref[h], vb, preferred_element_type=jnp.float32)
        carries = []
        for h in heads:
            carry = jnp.zeros((tq, 1), jnp.float32) if first else c_ref[h]
            a_ref[h], carry = weights(n[h], carry, strict)
            c_ref[h] = carry
            carries.append(carry)
            nz_ref[h] = n_next[h]
        return jnp.max(functools.reduce(jnp.maximum, carries))

    n0 = scores(i)
    for h in heads:
        nz_ref[h] = n0[h]
    acc_ref[...] = jnp.zeros_like(acc_ref)
    row = lax.broadcasted_iota(jnp.int32, (tq, tq), 0)
    col = lax.broadcasted_iota(jnp.int32, (tq, tq), 1)
    stick = step(i, col < row, True)

    def cond(state):
        t, stick = state
        return jnp.logical_and(t < i, stick > UNDERFLOW_LOG2)

    def body(state):
        t, _ = state
        return t + 1, step(i - 1 - t, None, False)

    done, _ = lax.while_loop(cond, body, (jnp.int32(0), stick))
    vb = tile(v_ref, i - done)
    out = [acc_ref[h] + jnp.dot(a_ref[h], vb, preferred_element_type=jnp.float32)
           for h in heads]
    res = out[0]
    for h in heads[1:]:
        res = jnp.where((lane // HEAD_DIM) == h, out[h], res)
    o_ref[0] = res.astype(o_ref.dtype)


def _stick_breaking(qkvg, batch, seq):
    tq = ATT_BLOCK
    width = SB_HEADS * HEAD_DIM
    blocks = D_MODEL // width
    tri = (jnp.arange(tq)[:, None] >= jnp.arange(tq)[None, :]).astype(jnp.bfloat16)
    tri2 = jnp.concatenate([tri, tri], axis=0)
    return pl.pallas_call(
        _sb_kernel,
        grid=(batch, blocks, seq // tq),
        in_specs=[pl.BlockSpec((1, tq, width), lambda b, h, i: (b, i, h)),
                  pl.BlockSpec((1, seq, width), lambda b, h, i: (b, 0, blocks + h)),
                  pl.BlockSpec((1, seq, width), lambda b, h, i: (b, 0, 2 * blocks + h)),
                  pl.BlockSpec((2 * tq, tq), lambda b, h, i: (0, 0))],
        out_specs=pl.BlockSpec((1, tq, width), lambda b, h, i: (b, i, h)),
        out_shape=jax.ShapeDtypeStruct((batch, seq, D_MODEL), jnp.bfloat16),
        scratch_shapes=[pltpu.VMEM((SB_HEADS, tq, width), jnp.float32),
                        pltpu.VMEM((SB_HEADS, tq, 1), jnp.float32),
                        pltpu.VMEM((SB_HEADS, tq, tq), jnp.float32),
                        pltpu.VMEM((SB_HEADS, tq, tq), jnp.bfloat16)],
        compiler_params=_params("parallel", "parallel", "arbitrary"),
        name="stick_breaking",
    )(qkvg, qkvg, qkvg, tri2)


def _diff_kernel(q_ref, k_ref, v_ref, lam_ref, sg_ref, o_ref, acc_ref, m_ref, alpha_ref, s_ref,
                 p_ref, *, lam_init):
    tq, tk = DIFF_BLOCK, DIFF_KEYS
    i = pl.program_id(2)
    jd = (i * tq) // tk
    comps = range(2)
    lane = lax.broadcasted_iota(jnp.int32, (1, LANES), 1)
    qs = q_ref[0]
    qc = [jnp.where((lane // HEAD_DIM) == c, qs, jnp.zeros_like(qs)) for c in comps]
    ones = jnp.ones((tk, LANES), jnp.bfloat16)

    def tile(ref, j):
        return ref[0, pl.ds(pl.multiple_of(j * tk, tk), tk), :]

    def scores(j):
        kb = tile(k_ref, j)
        return [lax.dot_general(qc[c], kb, _NT, preferred_element_type=jnp.float32)
                for c in comps]

    def accumulate(acc, alpha, p, j):
        v1 = jnp.concatenate([tile(v_ref, j), ones], axis=1)
        return [jnp.concatenate([alpha[c]] * 2, axis=1) * acc[c]
                + jnp.dot(p[c], v1, preferred_element_type=jnp.float32) for c in comps]

    def softmax_tile(s, m, causal):
        p, alpha, m_out = [], [], []
        for c in comps:
            sc = s[c] if causal is None else jnp.where(causal, s[c], NEG_BIG)
            m_new = jnp.maximum(m[c], jnp.broadcast_to(jnp.max(sc, axis=-1, keepdims=True),
                                                       m[c].shape))
            alpha.append(jnp.exp2(m[c] - m_new))
            p.append(jnp.exp2(sc - jnp.concatenate([m_new] * (tk // LANES), axis=1)
                              ).astype(jnp.bfloat16))
            m_out.append(m_new)
        return p, alpha, m_out

    def load(ref):
        return [ref[c] for c in comps]

    def store(ref, vals):
        for c in comps:
            ref[c] = vals[c]

    def step(j):
        s_next = scores(jnp.maximum(j - 1, 0))
        acc = accumulate(load(acc_ref), load(alpha_ref), load(p_ref), j + 1)
        p, alpha, m = softmax_tile(load(s_ref), load(m_ref), None)
        store(acc_ref, acc), store(p_ref, p), store(alpha_ref, alpha), store(m_ref, m)
        store(s_ref, s_next)

    row = lax.broadcasted_iota(jnp.int32, (tq, tk), 0)
    col = lax.broadcasted_iota(jnp.int32, (tq, tk), 1)
    s_diag = scores(jd)
    s_next = scores(jnp.maximum(jd - 1, 0))
    m0 = [jnp.full((tq, LANES), NEG_BIG, jnp.float32) for c in comps]
    p, alpha, m = softmax_tile(s_diag, m0, col <= row + (i * tq - jd * tk))
    store(p_ref, p), store(alpha_ref, alpha), store(m_ref, m), store(s_ref, s_next)
    acc_ref[...] = jnp.zeros_like(acc_ref)

    def body(t, _):
        step(jd - 1 - t)
        return 0

    lax.fori_loop(0, jd, body, 0)
    acc = accumulate(load(acc_ref), load(alpha_ref), load(p_ref), 0)

    lp = lam_ref[...]
    lam = (jnp.exp(jnp.sum(lp[0:1] * lp[1:2], axis=-1, keepdims=True))
           - jnp.exp(jnp.sum(lp[2:3] * lp[3:4], axis=-1, keepdims=True)) + lam_init)
    o = (acc[0][:, :LANES] / acc[0][:, LANES:] - lam * (acc[1][:, :LANES] / acc[1][:, LANES:]))
    o = _rms_scale(o) * sg_ref[...] * (1.0 - lam_init)
    o_ref[0] = o.astype(o_ref.dtype)


def _diff_attention(proj, lam_params, subln_g, lam_init, batch, seq):
    tq = DIFF_BLOCK
    heads = D_MODEL // LANES
    return pl.pallas_call(
        functools.partial(_diff_kernel, lam_init=lam_init),
        grid=(batch, heads, seq // tq),
        in_specs=[pl.BlockSpec((1, tq, LANES), lambda b, h, i: (b, i, 2 * heads + h)),
                  pl.BlockSpec((1, seq, LANES), lambda b, h, i: (b, 0, h)),
                  pl.BlockSpec((1, seq, LANES), lambda b, h, i: (b, 0, heads + h)),
                  pl.BlockSpec((4, HEAD_DIM), lambda b, h, i: (0, 0)),
                  pl.BlockSpec((1, LANES), lambda b, h, i: (0, 0))],
        out_specs=pl.BlockSpec((1, tq, LANES), lambda b, h, i: (b, i, h)),
        out_shape=jax.ShapeDtypeStruct((batch, seq, D_MODEL), jnp.bfloat16),
        scratch_shapes=[pltpu.VMEM((2, tq, 2 * LANES), jnp.float32),
                        pltpu.VMEM((2, tq, LANES), jnp.float32),
                        pltpu.VMEM((2, tq, LANES), jnp.float32),
                        pltpu.VMEM((2, tq, DIFF_KEYS), jnp.float32),
                        pltpu.VMEM((2, tq, DIFF_KEYS), jnp.bfloat16)],
        compiler_params=_params("parallel", "parallel", "arbitrary"),
        name="diff_attention",
    )(proj, proj, proj, lam_params, subln_g.reshape(1, LANES))


def kernel(x, positions, a_norm_g, a_w_in, a_w_out, kv_norm_g, w_kv, k_norm_g, b_norm_g,
           b_w_in, b_q_norm_g, b_lambda, b_subln_g, b_w_out):
    batch, seq, d = x.shape
    assert d == D_MODEL and seq % ATT_BLOCK == 0 and seq % DIFF_BLOCK == 0
    assert (batch * seq) % PROJ_ROWS == 0
    assert a_norm_g.shape[0] == 1 and b_norm_g.shape[0] == 1
    bf16 = jnp.bfloat16
    x2 = x.reshape(batch * seq, d)

    qkvg = _proj_a(x2, a_norm_g[0], a_w_in[0].astype(bf16))
    o = _stick_breaking(qkvg.reshape(batch, seq, 4 * d), batch, seq)
    x2 = _out_proj(qkvg, 3, o.reshape(batch * seq, d), x2, a_w_out[0].astype(bf16))

    cos, sin = _rope_tables(positions)
    w_b = jnp.concatenate([w_kv, b_w_in[0]], axis=1).astype(bf16)
    g2 = jnp.stack([kv_norm_g, b_norm_g[0]]).reshape(2, 1, d)
    ng2 = jnp.stack([jnp.tile(k_norm_g, d // HEAD_DIM),
                     jnp.tile(b_q_norm_g[0], d // HEAD_DIM)]).reshape(2, 1, d)
    proj = _proj_b(x2, g2, w_b, ng2, cos, sin)
    layer_idx = N_A_LAYERS
    lam_init = 0.8 - 0.6 * math.exp(-0.3 * layer_idx)
    o = _diff_attention(proj.reshape(batch, seq, 4 * d), b_lambda[0], b_subln_g[0], lam_init,
                        batch, seq)
    x2 = _out_proj(proj, 3, o.reshape(batch * seq, d), x2, b_w_out[0].astype(bf16))
    return x2.reshape(batch, seq, d)
```

```python
import functools
import math

import jax
import jax.numpy as jnp
from jax import lax
from jax.experimental import pallas as pl
from jax.experimental.pallas import tpu as pltpu

D_MODEL = 1024
HEAD_DIM = 64
LANES = 128
ROPE_THETA = 10000.0
EPS = 1e-6
DEPTH = 2
N_A_LAYERS = DEPTH // 2
VMEM_LIMIT_BYTES = 56 * 1024 * 1024

PROJ_ROWS = 512
PROJ_COLS = 1024
ATT_BLOCK = 256
SB_HEADS = 4
DIFF_BLOCK = 512
ONES_ROWS = 16
NEG_BIG = -1e30
LOG2E = math.log2(math.e)
DIFF_Q_SCALE = LOG2E / math.sqrt(HEAD_DIM)
SB_Q_SCALE = -LOG2E / math.sqrt(HEAD_DIM)
UNDERFLOW_LOG2 = -150.0

_NT = (((1,), (1,)), ((), ()))


def _params(*sem):
    return pltpu.CompilerParams(dimension_semantics=sem, vmem_limit_bytes=VMEM_LIMIT_BYTES)


def _rope_kernel(pos_ref, freq_ref, cos_ref, sin_ref):
    ang = pos_ref[...].astype(jnp.float32) * freq_ref[...]
    lane = lax.broadcasted_iota(jnp.int32, ang.shape, 1)
    first_half = (lane % HEAD_DIM) < (HEAD_DIM // 2)
    cos_ref[...] = jnp.cos(ang)
    sin = jnp.sin(ang)
    sin_ref[...] = jnp.where(first_half, -sin, sin)


def _rope_tables(positions):
    n = positions.size
    rows = 1024
    inv_freq = ROPE_THETA ** (-jnp.arange(0, HEAD_DIM, 2, dtype=jnp.float32) / HEAD_DIM)
    freq = jnp.tile(inv_freq, LANES // (HEAD_DIM // 2)).reshape(1, LANES)
    out = jax.ShapeDtypeStruct((n, LANES), jnp.float32)
    return pl.pallas_call(
        _rope_kernel,
        grid=(n // rows,),
        in_specs=[pl.BlockSpec((rows, 1), lambda i: (i, 0)),
                  pl.BlockSpec((1, LANES), lambda i: (0, 0))],
        out_specs=[pl.BlockSpec((rows, LANES), lambda i: (i, 0))] * 2,
        out_shape=[out, out],
        compiler_params=_params("parallel"),
        name="rope_tables",
    )(positions.reshape(n, 1), freq)


def _rms_scale(x):
    return x * lax.rsqrt(jnp.mean(x * x, axis=-1, keepdims=True) + EPS)


def _proj_a_kernel(x_ref, g_ref, w_ref, o_ref):
    h = (_rms_scale(x_ref[...]) * g_ref[...]).astype(jnp.bfloat16)
    for c in range(o_ref.shape[1] // PROJ_COLS):
        cols = slice(c * PROJ_COLS, (c + 1) * PROJ_COLS)
        acc = jnp.dot(h, w_ref[:, cols], preferred_element_type=jnp.float32)
        if c == 0:
            acc = acc * SB_Q_SCALE
        o_ref[:, cols] = acc.astype(o_ref.dtype)


def _proj_a(x2, g, w):
    m, d = x2.shape
    n = w.shape[1]
    return pl.pallas_call(
        _proj_a_kernel,
        grid=(m // PROJ_ROWS,),
        in_specs=[pl.BlockSpec((PROJ_ROWS, d), lambda i: (i, 0)),
                  pl.BlockSpec((1, d), lambda i: (0, 0)),
                  pl.BlockSpec((d, n), lambda i: (0, 0))],
        out_specs=pl.BlockSpec((PROJ_ROWS, n), lambda i: (i, 0)),
        out_shape=jax.ShapeDtypeStruct((m, n), jnp.bfloat16),
        compiler_params=_params("parallel"),
        name="proj_a",
    )(x2, g.reshape(1, d), w)


def _split_bf16(x):
    hi = x.astype(jnp.bfloat16)
    lo = (x - hi.astype(jnp.float32)).astype(jnp.bfloat16)
    return hi, lo


def _head_norm_rope(acc, ng, cos, sin, seg_mean):
    width = seg_mean.shape[0]
    lane = lax.broadcasted_iota(jnp.int32, cos.shape, 1)
    first_half = (lane % HEAD_DIM) < (HEAD_DIM // 2)
    outs = []
    for c in range(acc.shape[1] // width):
        t = acc[:, c * width:(c + 1) * width]
        hi, lo = _split_bf16(t * t)
        ms = (jnp.dot(hi, seg_mean, preferred_element_type=jnp.float32)
              + jnp.dot(lo, seg_mean, preferred_element_type=jnp.float32))
        tn = t * lax.rsqrt(ms + EPS) * ng[:, c * width:(c + 1) * width]
        for s in range(width // LANES):
            u = tn[:, s * LANES:(s + 1) * LANES]
            swapped = jnp.where(first_half,
                                pltpu.roll(u, LANES - HEAD_DIM // 2, axis=1),
                                pltpu.roll(u, HEAD_DIM // 2, axis=1))
            outs.append(u * cos + swapped * sin)
    return jnp.concatenate(outs, axis=1)


def _proj_b_kernel(x_ref, g_ref, w_ref, ng_ref, cos_ref, sin_ref, seg_ref, o_ref, vt_ref):
    xn = _rms_scale(x_ref[...])
    for c in range(o_ref.shape[1] // PROJ_COLS):
        cols = slice(c * PROJ_COLS, (c + 1) * PROJ_COLS)
        h = (xn * g_ref[c // 2]).astype(jnp.bfloat16)
        acc = jnp.dot(h, w_ref[:, cols], preferred_element_type=jnp.float32)
        if c % 2 == 0:
            acc = _head_norm_rope(acc, ng_ref[c // 2], cos_ref[...], sin_ref[...], seg_ref[...])
        if c == 1:
            vt_ref[0] = acc.T.astype(vt_ref.dtype)
        if c == 2:
            acc = acc * DIFF_Q_SCALE
        o_ref[:, cols] = acc.astype(o_ref.dtype)


def _proj_b(x2, g2, w, ng2, cos, sin, batch):
    m, d = x2.shape
    n = w.shape[1]
    seq = m // batch
    tiles = seq // PROJ_ROWS
    width = 2 * LANES
    seg = (jnp.arange(width)[:, None] // HEAD_DIM == jnp.arange(width)[None, :] // HEAD_DIM)
    seg_mean = (seg.astype(jnp.float32) / HEAD_DIM).astype(jnp.bfloat16)
    return pl.pallas_call(
        _proj_b_kernel,
        grid=(m // PROJ_ROWS,),
        in_specs=[pl.BlockSpec((PROJ_ROWS, d), lambda i: (i, 0)),
                  pl.BlockSpec((2, 1, d), lambda i: (0, 0, 0)),
                  pl.BlockSpec((d, n), lambda i: (0, 0)),
                  pl.BlockSpec((2, 1, PROJ_COLS), lambda i: (0, 0, 0)),
                  pl.BlockSpec((PROJ_ROWS, LANES), lambda i: (i, 0)),
                  pl.BlockSpec((PROJ_ROWS, LANES), lambda i: (i, 0)),
                  pl.BlockSpec((width, width), lambda i: (0, 0))],
        out_specs=[pl.BlockSpec((PROJ_ROWS, n), lambda i: (i, 0)),
                   pl.BlockSpec((1, d, PROJ_ROWS), lambda i: (i // tiles, 0, i % tiles))],
        out_shape=[jax.ShapeDtypeStruct((m, n), jnp.bfloat16),
                   jax.ShapeDtypeStruct((batch, d, seq), jnp.bfloat16)],
        compiler_params=_params("parallel"),
        name="proj_b",
    )(x2, g2, w, ng2, cos, sin, seg_mean)


def _out_proj_kernel(gate_ref, o_ref, x_ref, w_ref, y_ref):
    gate = gate_ref[...].astype(jnp.float32)
    u = (gate * jax.nn.sigmoid(gate) * o_ref[...].astype(jnp.float32)).astype(jnp.bfloat16)
    y_ref[...] = x_ref[...] + jnp.dot(u, w_ref[...], preferred_element_type=jnp.float32)


def _out_proj(proj, gate_col_block, o, x2, w):
    m, d = x2.shape
    return pl.pallas_call(
        _out_proj_kernel,
        grid=(m // PROJ_ROWS,),
        in_specs=[pl.BlockSpec((PROJ_ROWS, d), lambda i: (i, gate_col_block)),
                  pl.BlockSpec((PROJ_ROWS, d), lambda i: (i, 0)),
                  pl.BlockSpec((PROJ_ROWS, d), lambda i: (i, 0)),
                  pl.BlockSpec((d, d), lambda i: (0, 0))],
        out_specs=pl.BlockSpec((PROJ_ROWS, d), lambda i: (i, 0)),
        out_shape=jax.ShapeDtypeStruct((m, d), jnp.float32),
        compiler_params=_params("parallel"),
        name="out_proj",
    )(proj, o, x2, w)


def _sb_kernel(q_ref, k_ref, v_ref, tri_ref, o_ref, acc_ref, c_ref, nz_ref, a_ref):
    tq = ATT_BLOCK
    i = pl.program_id(2)
    heads = range(SB_HEADS)
    lane = lax.broadcasted_iota(jnp.int32, (1, SB_HEADS * HEAD_DIM), 1)
    qn = q_ref[0]
    qh = [jnp.where((lane // HEAD_DIM) == h, qn, jnp.zeros_like(qn)) for h in heads]
    tri2 = tri_ref[...]

    def tile(ref, j):
        return ref[0, pl.ds(pl.multiple_of(j * tq, tq), tq), :]

    def scores(j):
        kb = tile(k_ref, j)
        return [lax.dot_general(qh[h], kb, _NT, preferred_element_type=jnp.float32)
                for h in heads]

    def weights(n, carry, strict):
        l = jnp.minimum(n, 0.0) - jnp.log2(1.0 + jnp.exp2(-jnp.abs(n)))
        if strict is not None:
            l = jnp.where(strict, l, 0.0)
        hi, lo = _split_bf16(l)
        incl = jnp.dot(jnp.concatenate([hi, lo], axis=1), tri2,
                       preferred_element_type=jnp.float32)
        a = jnp.exp2(incl - n + carry)
        if strict is not None:
            a = jnp.where(strict, a, 0.0)
        return a.astype(jnp.bfloat16), carry + incl[:, :1]

    def step(j, strict, first):
        n_next = scores(jnp.maximum(j - 1, 0))
        n = [nz_ref[h] for h in heads]
        if not first:
            vb = tile(v_ref, j + 1)
            for h in heads:
                acc_ref[h] += jnp.dot(a_ref[h], vb, preferred_element_type=jnp.float32)
        carries = []
        for h in heads:
            carry = jnp.zeros((tq, 1), jnp.float32) if first else c_ref[h]
            a_ref[h], carry = weights(n[h], carry, strict)
            c_ref[h] = carry
            carries.append(carry)
            nz_ref[h] = n_next[h]
        return jnp.max(functools.reduce(jnp.maximum, carries))

    n0 = scores(i)
    for h in heads:
        nz_ref[h] = n0[h]
    acc_ref[...] = jnp.zeros_like(acc_ref)
    row = lax.broadcasted_iota(jnp.int32, (tq, tq), 0)
    col = lax.broadcasted_iota(jnp.int32, (tq, tq), 1)
    stick = step(i, col < row, True)

    def cond(state):
        t, stick = state
        return jnp.logical_and(t < i, stick > UNDERFLOW_LOG2)

    def body(state):
        t, _ = state
        return t + 1, step(i - 1 - t, None, False)

    done, _ = lax.while_loop(cond, body, (jnp.int32(0), stick))
    vb = tile(v_ref, i - done)
    out = [acc_ref[h] + jnp.dot(a_ref[h], vb, preferred_element_type=jnp.float32)
           for h in heads]
    res = out[0]
    for h in heads[1:]:
        res = jnp.where((lane // HEAD_DIM) == h, out[h], res)
    o_ref[0] = res.astype(o_ref.dtype)


def _stick_breaking(qkvg, batch, seq):
    tq = ATT_BLOCK
    width = SB_HEADS * HEAD_DIM
    blocks = D_MODEL // width
    tri = (jnp.arange(tq)[:, None] >= jnp.arange(tq)[None, :]).astype(jnp.bfloat16)
    tri2 = jnp.concatenate([tri, tri], axis=0)
    return pl.pallas_call(
        _sb_kernel,
        grid=(batch, blocks, seq // tq),
        in_specs=[pl.BlockSpec((1, tq, width), lambda b, h, i: (b, i, h)),
                  pl.BlockSpec((1, seq, width), lambda b, h, i: (b, 0, blocks + h)),
                  pl.BlockSpec((1, seq, width), lambda b, h, i: (b, 0, 2 * blocks + h)),
                  pl.BlockSpec((2 * tq, tq), lambda b, h, i: (0, 0))],
        out_specs=pl.BlockSpec((1, tq, width), lambda b, h, i: (b, i, h)),
        out_shape=jax.ShapeDtypeStruct((batch, seq, D_MODEL), jnp.bfloat16),
        scratch_shapes=[pltpu.VMEM((SB_HEADS, tq, width), jnp.float32),
                        pltpu.VMEM((SB_HEADS, tq, 1), jnp.float32),
                        pltpu.VMEM((SB_HEADS, tq, tq), jnp.float32),
                        pltpu.VMEM((SB_HEADS, tq, tq), jnp.bfloat16)],
        compiler_params=_params("parallel", "parallel", "arbitrary"),
        name="stick_breaking",
    )(qkvg, qkvg, qkvg, tri2)


def _diff_kernel(q_ref, k_ref, vt_ref, lam_ref, sg_ref, o_ref, acc_ref, m_ref, alpha_ref, s_ref,
                 p_ref, *, lam_init):
    tq = tk = DIFF_BLOCK
    i = pl.program_id(2)
    comps = range(2)
    lane = lax.broadcasted_iota(jnp.int32, (1, LANES), 1)
    qs = q_ref[0]
    qc = [jnp.where((lane // HEAD_DIM) == c, qs, jnp.zeros_like(qs)) for c in comps]
    ones = jnp.ones((ONES_ROWS, tk), jnp.bfloat16)

    def scores(j):
        kb = k_ref[0, pl.ds(pl.multiple_of(j * tk, tk), tk), :]
        return [lax.dot_general(kb, qc[c], _NT, preferred_element_type=jnp.float32)
                for c in comps]

    def accumulate(acc, alpha, p, j):
        vt = vt_ref[0, :, pl.ds(pl.multiple_of(j * tk, tk), tk)]
        vt1 = jnp.concatenate([vt, ones], axis=0)
        return [alpha[c] * acc[c] + jnp.dot(vt1, p[c], preferred_element_type=jnp.float32)
                for c in comps]

    def softmax_tile(s, m, causal):
        p, alpha, m_out = [], [], []
        for c in comps:
            sc = s[c] if causal is None else jnp.where(causal, s[c], NEG_BIG)
            m_new = jnp.maximum(m[c], jnp.max(sc, axis=0, keepdims=True))
            alpha.append(jnp.exp2(m[c] - m_new))
            p.append(jnp.exp2(sc - m_new).astype(jnp.bfloat16))
            m_out.append(m_new)
        return p, alpha, m_out

    def load(ref):
        return [ref[c] for c in comps]

    def store(ref, vals):
        for c in comps:
            ref[c] = vals[c]

    def step(j):
        s_next = scores(jnp.maximum(j - 1, 0))
        acc = accumulate(load(acc_ref), load(alpha_ref), load(p_ref), j + 1)
        p, alpha, m = softmax_tile(load(s_ref), load(m_ref), None)
        store(acc_ref, acc), store(p_ref, p), store(alpha_ref, alpha), store(m_ref, m)
        store(s_ref, s_next)

    key = lax.broadcasted_iota(jnp.int32, (tk, tq), 0)
    qry = lax.broadcasted_iota(jnp.int32, (tk, tq), 1)
    s_diag = scores(i)
    s_next = scores(jnp.maximum(i - 1, 0))
    m0 = [jnp.full((1, tq), NEG_BIG, jnp.float32) for c in comps]
    p, alpha, m = softmax_tile(s_diag, m0, key <= qry)
    store(p_ref, p), store(alpha_ref, alpha), store(m_ref, m), store(s_ref, s_next)
    acc_ref[...] = jnp.zeros_like(acc_ref)

    def body(t, _):
        step(i - 1 - t)
        return 0

    lax.fori_loop(0, i, body, 0)
    acc = accumulate(load(acc_ref), load(alpha_ref), load(p_ref), 0)

    lp = lam_ref[...]
    lam = (jnp.exp(jnp.sum(lp[0:1] * lp[1:2], axis=-1, keepdims=True))
           - jnp.exp(jnp.sum(lp[2:3] * lp[3:4], axis=-1, keepdims=True)) + lam_init)
    num = [acc[c][:LANES] for c in comps]
    den = [acc[c][LANES:LANES + 1] for c in comps]
    ot = num[0] / den[0] - lam * (num[1] / den[1])
    ot = ot * lax.rsqrt(jnp.mean(ot * ot, axis=0, keepdims=True) + EPS)
    o = ot.T * sg_ref[...] * (1.0 - lam_init)
    o_ref[0] = o.astype(o_ref.dtype)


def _diff_attention(proj, vt, lam_params, subln_g, lam_init, batch, seq):
    tq = DIFF_BLOCK
    heads = D_MODEL // LANES
    rows = LANES + ONES_ROWS
    return pl.pallas_call(
        functools.partial(_diff_kernel, lam_init=lam_init),
        grid=(batch, heads, seq // tq),
        in_specs=[pl.BlockSpec((1, tq, LANES), lambda b, h, i: (b, i, 2 * heads + h)),
                  pl.BlockSpec((1, seq, LANES), lambda b, h, i: (b, 0, h)),
                  pl.BlockSpec((1, LANES, seq), lambda b, h, i: (b, h, 0)),
                  pl.BlockSpec((4, HEAD_DIM), lambda b, h, i: (0, 0)),
                  pl.BlockSpec((1, LANES), lambda b, h, i: (0, 0))],
        out_specs=pl.BlockSpec((1, tq, LANES), lambda b, h, i: (b, i, h)),
        out_shape=jax.ShapeDtypeStruct((batch, seq, D_MODEL), jnp.bfloat16),
        scratch_shapes=[pltpu.VMEM((2, rows, tq), jnp.float32),
                        pltpu.VMEM((2, 1, tq), jnp.float32),
                        pltpu.VMEM((2, 1, tq), jnp.float32),
                        pltpu.VMEM((2, tq, tq), jnp.float32),
                        pltpu.VMEM((2, tq, tq), jnp.bfloat16)],
        compiler_params=_params("parallel", "parallel", "arbitrary"),
        name="diff_attention",
    )(proj, proj, vt, lam_params, subln_g.reshape(1, LANES))


def kernel(x, positions, a_norm_g, a_w_in, a_w_out, kv_norm_g, w_kv, k_norm_g, b_norm_g,
           b_w_in, b_q_norm_g, b_lambda, b_subln_g, b_w_out):
    batch, seq, d = x.shape
    assert d == D_MODEL and seq % ATT_BLOCK == 0 and seq % DIFF_BLOCK == 0
    assert (batch * seq) % PROJ_ROWS == 0
    assert a_norm_g.shape[0] == 1 and b_norm_g.shape[0] == 1
    bf16 = jnp.bfloat16
    x2 = x.reshape(batch * seq, d)

    qkvg = _proj_a(x2, a_norm_g[0], a_w_in[0].astype(bf16))
    o = _stick_breaking(qkvg.reshape(batch, seq, 4 * d), batch, seq)
    x2 = _out_proj(qkvg, 3, o.reshape(batch * seq, d), x2, a_w_out[0].astype(bf16))

    cos, sin = _rope_tables(positions)
    w_b = jnp.concatenate([w_kv, b_w_in[0]], axis=1).astype(bf16)
    g2 = jnp.stack([kv_norm_g, b_norm_g[0]]).reshape(2, 1, d)
    ng2 = jnp.stack([jnp.tile(k_norm_g, d // HEAD_DIM),
                     jnp.tile(b_q_norm_g[0], d // HEAD_DIM)]).reshape(2, 1, d)
    proj, vt = _proj_b(x2, g2, w_b, ng2, cos, sin, batch)
    layer_idx = N_A_LAYERS
    lam_init = 0.8 - 0.6 * math.exp(-0.3 * layer_idx)
    o = _diff_attention(proj.reshape(batch, seq, 4 * d), vt, b_lambda[0], b_subln_g[0],
                        lam_init, batch, seq)
    x2 = _out_proj(proj, 3, o.reshape(batch * seq, d), x2, b_w_out[0].astype(bf16))
    return x2.reshape(batch, seq, d)
```

```python
import functools
import math

import jax
import jax.numpy as jnp
import numpy as np
from jax import lax
from jax.experimental import pallas as pl
from jax.experimental.pallas import tpu as pltpu

D_MODEL = 1024
HEAD_DIM = 64
LANES = 128
ROPE_THETA = 10000.0
EPS = 1e-6
DEPTH = 2
N_A_LAYERS = DEPTH // 2
VMEM_LIMIT_BYTES = 56 * 1024 * 1024

PROJ_ROWS = 512
PROJ_COLS = 1024
ATT_BLOCK = 256
SB_HEADS = 4
DIFF_BLOCK = 512
ONES_ROWS = 16
NEG_BIG = -1e30
LOG2E = math.log2(math.e)
DIFF_Q_SCALE = LOG2E / math.sqrt(HEAD_DIM)
SB_Q_SCALE = -LOG2E / math.sqrt(HEAD_DIM)
UNDERFLOW_LOG2 = -150.0

_NT = (((1,), (1,)), ((), ()))


def _params(*sem):
    return pltpu.CompilerParams(dimension_semantics=sem, vmem_limit_bytes=VMEM_LIMIT_BYTES)


def _rope_kernel(pos_ref, freq_ref, cos_ref, sin_ref):
    ang = pos_ref[...].astype(jnp.float32) * freq_ref[...]
    lane = lax.broadcasted_iota(jnp.int32, ang.shape, 1)
    first_half = (lane % HEAD_DIM) < (HEAD_DIM // 2)
    cos_ref[...] = jnp.cos(ang)
    sin = jnp.sin(ang)
    sin_ref[...] = jnp.where(first_half, -sin, sin)


def _rope_tables(positions):
    n = positions.size
    rows = 1024
    inv_freq = ROPE_THETA ** (-jnp.arange(0, HEAD_DIM, 2, dtype=jnp.float32) / HEAD_DIM)
    freq = jnp.tile(inv_freq, LANES // (HEAD_DIM // 2)).reshape(1, LANES)
    out = jax.ShapeDtypeStruct((n, LANES), jnp.float32)
    return pl.pallas_call(
        _rope_kernel,
        grid=(n // rows,),
        in_specs=[pl.BlockSpec((rows, 1), lambda i: (i, 0)),
                  pl.BlockSpec((1, LANES), lambda i: (0, 0))],
        out_specs=[pl.BlockSpec((rows, LANES), lambda i: (i, 0))] * 2,
        out_shape=[out, out],
        compiler_params=_params("parallel"),
        name="rope_tables",
    )(positions.reshape(n, 1), freq)


def _rms_scale(x):
    return x * lax.rsqrt(jnp.mean(x * x, axis=-1, keepdims=True) + EPS)


def _proj_a_kernel(x_ref, g_ref, w_ref, o_ref):
    h = (_rms_scale(x_ref[...]) * g_ref[...]).astype(jnp.bfloat16)
    for c in range(o_ref.shape[1] // PROJ_COLS):
        cols = slice(c * PROJ_COLS, (c + 1) * PROJ_COLS)
        acc = jnp.dot(h, w_ref[:, cols], preferred_element_type=jnp.float32)
        if c == 0:
            acc = acc * SB_Q_SCALE
        o_ref[:, cols] = acc.astype(o_ref.dtype)


def _proj_a(x2, g, w):
    m, d = x2.shape
    n = w.shape[1]
    return pl.pallas_call(
        _proj_a_kernel,
        grid=(m // PROJ_ROWS,),
        in_specs=[pl.BlockSpec((PROJ_ROWS, d), lambda i: (i, 0)),
                  pl.BlockSpec((1, d), lambda i: (0, 0)),
                  pl.BlockSpec((d, n), lambda i: (0, 0))],
        out_specs=pl.BlockSpec((PROJ_ROWS, n), lambda i: (i, 0)),
        out_shape=jax.ShapeDtypeStruct((m, n), jnp.bfloat16),
        compiler_params=_params("parallel"),
        name="proj_a",
    )(x2, g.reshape(1, d), w)


def _split_bf16(x):
    hi = x.astype(jnp.bfloat16)
    lo = (x - hi.astype(jnp.float32)).astype(jnp.bfloat16)
    return hi, lo


def _head_norm_rope(acc, ng, cos, sin, seg_mean):
    width = seg_mean.shape[0]
    lane = lax.broadcasted_iota(jnp.int32, cos.shape, 1)
    first_half = (lane % HEAD_DIM) < (HEAD_DIM // 2)
    outs = []
    for c in range(acc.shape[1] // width):
        t = acc[:, c * width:(c + 1) * width]
        hi, lo = _split_bf16(t * t)
        ms = (jnp.dot(hi, seg_mean, preferred_element_type=jnp.float32)
              + jnp.dot(lo, seg_mean, preferred_element_type=jnp.float32))
        tn = t * lax.rsqrt(ms + EPS) * ng[:, c * width:(c + 1) * width]
        for s in range(width // LANES):
            u = tn[:, s * LANES:(s + 1) * LANES]
            swapped = jnp.where(first_half,
                                pltpu.roll(u, LANES - HEAD_DIM // 2, axis=1),
                                pltpu.roll(u, HEAD_DIM // 2, axis=1))
            outs.append(u * cos + swapped * sin)
    return jnp.concatenate(outs, axis=1)


def _proj_b_kernel(x_ref, g_ref, w_ref, ng_ref, cos_ref, sin_ref, seg_ref, o_ref, vt_ref):
    xn = _rms_scale(x_ref[...])
    for c in range(o_ref.shape[1] // PROJ_COLS):
        cols = slice(c * PROJ_COLS, (c + 1) * PROJ_COLS)
        h = (xn * g_ref[c // 2]).astype(jnp.bfloat16)
        acc = jnp.dot(h, w_ref[:, cols], preferred_element_type=jnp.float32)
        if c % 2 == 0:
            acc = _head_norm_rope(acc, ng_ref[c // 2], cos_ref[...], sin_ref[...], seg_ref[...])
        if c == 1:
            vt_ref[0] = acc.T.astype(vt_ref.dtype)
        if c == 2:
            acc = acc * DIFF_Q_SCALE
        o_ref[:, cols] = acc.astype(o_ref.dtype)


def _proj_b(x2, g2, w, ng2, cos, sin, batch):
    m, d = x2.shape
    n = w.shape[1]
    seq = m // batch
    tiles = seq // PROJ_ROWS
    width = 2 * LANES
    seg = np.arange(width)[:, None] // HEAD_DIM == np.arange(width)[None, :] // HEAD_DIM
    seg_mean = jnp.asarray(seg.astype(np.float32) / HEAD_DIM, jnp.bfloat16)
    return pl.pallas_call(
        _proj_b_kernel,
        grid=(m // PROJ_ROWS,),
        in_specs=[pl.BlockSpec((PROJ_ROWS, d), lambda i: (i, 0)),
                  pl.BlockSpec((2, 1, d), lambda i: (0, 0, 0)),
                  pl.BlockSpec((d, n), lambda i: (0, 0)),
                  pl.BlockSpec((2, 1, PROJ_COLS), lambda i: (0, 0, 0)),
                  pl.BlockSpec((PROJ_ROWS, LANES), lambda i: (i, 0)),
                  pl.BlockSpec((PROJ_ROWS, LANES), lambda i: (i, 0)),
                  pl.BlockSpec((width, width), lambda i: (0, 0))],
        out_specs=[pl.BlockSpec((PROJ_ROWS, n), lambda i: (i, 0)),
                   pl.BlockSpec((1, d, PROJ_ROWS), lambda i: (i // tiles, 0, i % tiles))],
        out_shape=[jax.ShapeDtypeStruct((m, n), jnp.bfloat16),
                   jax.ShapeDtypeStruct((batch, d, seq), jnp.bfloat16)],
        compiler_params=_params("parallel"),
        name="proj_b",
    )(x2, g2, w, ng2, cos, sin, seg_mean)


def _out_proj_kernel(gate_ref, o_ref, x_ref, w_ref, y_ref):
    gate = gate_ref[...].astype(jnp.float32)
    u = (gate * jax.nn.sigmoid(gate) * o_ref[...].astype(jnp.float32)).astype(jnp.bfloat16)
    y_ref[...] = x_ref[...] + jnp.dot(u, w_ref[...], preferred_element_type=jnp.float32)


def _out_proj(proj, gate_col_block, o, x2, w):
    m, d = x2.shape
    return pl.pallas_call(
        _out_proj_kernel,
        grid=(m // PROJ_ROWS,),
        in_specs=[pl.BlockSpec((PROJ_ROWS, d), lambda i: (i, gate_col_block)),
                  pl.BlockSpec((PROJ_ROWS, d), lambda i: (i, 0)),
                  pl.BlockSpec((PROJ_ROWS, d), lambda i: (i, 0)),
                  pl.BlockSpec((d, d), lambda i: (0, 0))],
        out_specs=pl.BlockSpec((PROJ_ROWS, d), lambda i: (i, 0)),
        out_shape=jax.ShapeDtypeStruct((m, d), jnp.float32),
        compiler_params=_params("parallel"),
        name="out_proj",
    )(proj, o, x2, w)


def _sb_kernel(q_ref, k_ref, v_ref, tri_ref, o_ref, acc_ref, c_ref, nz_ref, a_ref):
    tq = ATT_BLOCK
    i = pl.program_id(2)
    heads = range(SB_HEADS)
    lane = lax.broadcasted_iota(jnp.int32, (1, SB_HEADS * HEAD_DIM), 1)
    qn = q_ref[0]
    qh = [jnp.where((lane // HEAD_DIM) == h, qn, jnp.zeros_like(qn)) for h in heads]
    tri2 = tri_ref[...]

    def tile(ref, j):
        return ref[0, pl.ds(pl.multiple_of(j * tq, tq), tq), :]

    def scores(j):
        kb = tile(k_ref, j)
        return [lax.dot_general(qh[h], kb, _NT, preferred_element_type=jnp.float32)
                for h in heads]

    def weights(n, carry, strict):
        neg_abs = pltpu.bitcast(pltpu.bitcast(n, jnp.uint32) | jnp.uint32(1 << 31), jnp.float32)
        l = jnp.minimum(n, 0.0) - jnp.log2(1.0 + jnp.exp2(neg_abs))
        if strict is not None:
            l = jnp.where(strict, l, 0.0)
        hi, lo = _split_bf16(l)
        incl = jnp.dot(jnp.concatenate([hi, lo], axis=1), tri2,
                       preferred_element_type=jnp.float32)
        a = jnp.exp2(incl - n + carry)
        if strict is not None:
            a = jnp.where(strict, a, 0.0)
        return a.astype(jnp.bfloat16), carry + incl[:, :1]

    def step(j, strict, first):
        n_next = scores(jnp.maximum(j - 1, 0))
        n = [nz_ref[h] for h in heads]
        if not first:
            vb = tile(v_ref, j + 1)
            for h in heads:
                acc_ref[h] += jnp.dot(a_ref[h], vb, preferred_element_type=jnp.float32)
        carries = []
        for h in heads:
            carry = jnp.zeros((tq, 1), jnp.float32) if first else c_ref[h]
            a_ref[h], carry = weights(n[h], carry, strict)
            c_ref[h] = carry
            carries.append(carry)
            nz_ref[h] = n_next[h]
        return jnp.max(functools.reduce(jnp.maximum, carries))

    n0 = scores(i)
    for h in heads:
        nz_ref[h] = n0[h]
    acc_ref[...] = jnp.zeros_like(acc_ref)
    row = lax.broadcasted_iota(jnp.int32, (tq, tq), 0)
    col = lax.broadcasted_iota(jnp.int32, (tq, tq), 1)
    stick = step(i, col < row, True)

    def cond(state):
        t, stick = state
        return jnp.logical_and(t < i, stick > UNDERFLOW_LOG2)

    def body(state):
        t, _ = state
        return t + 1, step(i - 1 - t, None, False)

    done, _ = lax.while_loop(cond, body, (jnp.int32(0), stick))
    vb = tile(v_ref, i - done)
    out = [acc_ref[h] + jnp.dot(a_ref[h], vb, preferred_element_type=jnp.float32)
           for h in heads]
    res = out[0]
    for h in heads[1:]:
        res = jnp.where((lane // HEAD_DIM) == h, out[h], res)
    o_ref[0] = res.astype(o_ref.dtype)


def _stick_breaking(qkvg, batch, seq):
    tq = ATT_BLOCK
    width = SB_HEADS * HEAD_DIM
    blocks = D_MODEL // width
    tri = (np.arange(tq)[:, None] >= np.arange(tq)[None, :]).astype(np.float32)
    tri2 = jnp.asarray(np.concatenate([tri, tri], axis=0), jnp.bfloat16)
    return pl.pallas_call(
        _sb_kernel,
        grid=(batch, blocks, seq // tq),
        in_specs=[pl.BlockSpec((1, tq, width), lambda b, h, i: (b, i, h)),
                  pl.BlockSpec((1, seq, width), lambda b, h, i: (b, 0, blocks + h)),
                  pl.BlockSpec((1, seq, width), lambda b, h, i: (b, 0, 2 * blocks + h)),
                  pl.BlockSpec((2 * tq, tq), lambda b, h, i: (0, 0))],
        out_specs=pl.BlockSpec((1, tq, width), lambda b, h, i: (b, i, h)),
        out_shape=jax.ShapeDtypeStruct((batch, seq, D_MODEL), jnp.bfloat16),
        scratch_shapes=[pltpu.VMEM((SB_HEADS, tq, width), jnp.float32),
                        pltpu.VMEM((SB_HEADS, tq, 1), jnp.float32),
                        pltpu.VMEM((SB_HEADS, tq, tq), jnp.float32),
                        pltpu.VMEM((SB_HEADS, tq, tq), jnp.bfloat16)],
        compiler_params=_params("parallel", "parallel", "arbitrary"),
        name="stick_breaking",
    )(qkvg, qkvg, qkvg, tri2)


def _diff_kernel(q_ref, k_ref, vt_ref, lam_ref, sg_ref, o_ref, acc_ref, m_ref, alpha_ref, s_ref,
                 p_ref, *, lam_init):
    tq = tk = DIFF_BLOCK
    i = pl.program_id(2)
    comps = range(2)
    lane = lax.broadcasted_iota(jnp.int32, (1, LANES), 1)
    qs = q_ref[0]
    qc = [jnp.where((lane // HEAD_DIM) == c, qs, jnp.zeros_like(qs)) for c in comps]
    ones = jnp.ones((ONES_ROWS, tk), jnp.bfloat16)

    def scores(j):
        kb = k_ref[0, pl.ds(pl.multiple_of(j * tk, tk), tk), :]
        return [lax.dot_general(kb, qc[c], _NT, preferred_element_type=jnp.float32)
                for c in comps]

    def accumulate(acc, alpha, p, j):
        vt = vt_ref[0, :, pl.ds(pl.multiple_of(j * tk, tk), tk)]
        vt1 = jnp.concatenate([vt, ones], axis=0)
        return [alpha[c] * acc[c] + jnp.dot(vt1, p[c], preferred_element_type=jnp.float32)
                for c in comps]

    def softmax_tile(s, m, causal):
        p, alpha, m_out = [], [], []
        for c in comps:
            sc = s[c] if causal is None else jnp.where(causal, s[c], NEG_BIG)
            m_new = jnp.maximum(m[c], jnp.max(sc, axis=0, keepdims=True))
            alpha.append(jnp.exp2(m[c] - m_new))
            p.append(jnp.exp2(sc - m_new).astype(jnp.bfloat16))
            m_out.append(m_new)
        return p, alpha, m_out

    def load(ref):
        return [ref[c] for c in comps]

    def store(ref, vals):
        for c in comps:
            ref[c] = vals[c]

    def step(j):
        s_next = scores(jnp.maximum(j - 1, 0))
        acc = accumulate(load(acc_ref), load(alpha_ref), load(p_ref), j + 1)
        p, alpha, m = softmax_tile(load(s_ref), load(m_ref), None)
        store(acc_ref, acc), store(p_ref, p), store(alpha_ref, alpha), store(m_ref, m)
        store(s_ref, s_next)

    key = lax.broadcasted_iota(jnp.int32, (tk, tq), 0)
    qry = lax.broadcasted_iota(jnp.int32, (tk, tq), 1)
    s_diag = scores(i)
    s_next = scores(jnp.maximum(i - 1, 0))
    m0 = [jnp.full((1, tq), NEG_BIG, jnp.float32) for c in comps]
    p, alpha, m = softmax_tile(s_diag, m0, key <= qry)
    store(p_ref, p), store(alpha_ref, alpha), store(m_ref, m), store(s_ref, s_next)
    acc_ref[...] = jnp.zeros_like(acc_ref)

    def body(t, _):
        step(i - 1 - t)
        return 0

    lax.fori_loop(0, i, body, 0)
    acc = accumulate(load(acc_ref), load(alpha_ref), load(p_ref), 0)

    lp = lam_ref[...]
    lam = (jnp.exp(jnp.sum(lp[0:1] * lp[1:2], axis=-1, keepdims=True))
           - jnp.exp(jnp.sum(lp[2:3] * lp[3:4], axis=-1, keepdims=True)) + lam_init)
    num = [acc[c][:LANES] for c in comps]
    den = [acc[c][LANES:LANES + 1] for c in comps]
    ot = num[0] / den[0] - lam * (num[1] / den[1])
    ot = ot * lax.rsqrt(jnp.mean(ot * ot, axis=0, keepdims=True) + EPS)
    o = ot.T * sg_ref[...] * (1.0 - lam_init)
    o_ref[0] = o.astype(o_ref.dtype)


def _diff_attention(proj, vt, lam_params, subln_g, lam_init, batch, seq):
    tq = DIFF_BLOCK
    heads = D_MODEL // LANES
    rows = LANES + ONES_ROWS
    return pl.pallas_call(
        functools.partial(_diff_kernel, lam_init=lam_init),
        grid=(batch, heads, seq // tq),
        in_specs=[pl.BlockSpec((1, tq, LANES), lambda b, h, i: (b, i, 2 * heads + h)),
                  pl.BlockSpec((1, seq, LANES), lambda b, h, i: (b, 0, h)),
                  pl.BlockSpec((1, LANES, seq), lambda b, h, i: (b, h, 0)),
                  pl.BlockSpec((4, HEAD_DIM), lambda b, h, i: (0, 0)),
                  pl.BlockSpec((1, LANES), lambda b, h, i: (0, 0))],
        out_specs=pl.BlockSpec((1, tq, LANES), lambda b, h, i: (b, i, h)),
        out_shape=jax.ShapeDtypeStruct((batch, seq, D_MODEL), jnp.bfloat16),
        scratch_shapes=[pltpu.VMEM((2, rows, tq), jnp.float32),
                        pltpu.VMEM((2, 1, tq), jnp.float32),
                        pltpu.VMEM((2, 1, tq), jnp.float32),
                        pltpu.VMEM((2, tq, tq), jnp.float32),
                        pltpu.VMEM((2, tq, tq), jnp.bfloat16)],
        compiler_params=_params("parallel", "parallel", "arbitrary"),
        name="diff_attention",
    )(proj, proj, vt, lam_params, subln_g.reshape(1, LANES))


def kernel(x, positions, a_norm_g, a_w_in, a_w_out, kv_norm_g, w_kv, k_norm_g, b_norm_g,
           b_w_in, b_q_norm_g, b_lambda, b_subln_g, b_w_out):
    batch, seq, d = x.shape
    assert d == D_MODEL and seq % ATT_BLOCK == 0 and seq % DIFF_BLOCK == 0
    assert (batch * seq) % PROJ_ROWS == 0
    assert a_norm_g.shape[0] == 1 and b_norm_g.shape[0] == 1
    bf16 = jnp.bfloat16
    x2 = x.reshape(batch * seq, d)

    qkvg = _proj_a(x2, a_norm_g[0], a_w_in[0].astype(bf16))
    o = _stick_breaking(qkvg.reshape(batch, seq, 4 * d), batch, seq)
    x2 = _out_proj(qkvg, 3, o.reshape(batch * seq, d), x2, a_w_out[0].astype(bf16))

    cos, sin = _rope_tables(positions)
    w_b = jnp.concatenate([w_kv, b_w_in[0]], axis=1).astype(bf16)
    g2 = jnp.stack([kv_norm_g, b_norm_g[0]]).reshape(2, 1, d)
    ng2 = jnp.stack([jnp.tile(k_norm_g, d // HEAD_DIM),
                     jnp.tile(b_q_norm_g[0], d // HEAD_DIM)]).reshape(2, 1, d)
    proj, vt = _proj_b(x2, g2, w_b, ng2, cos, sin, batch)
    layer_idx = N_A_LAYERS
    lam_init = 0.8 - 0.6 * math.exp(-0.3 * layer_idx)
    o = _diff_attention(proj.reshape(batch, seq, 4 * d), vt, b_lambda[0], b_subln_g[0],
                        lam_init, batch, seq)
    x2 = _out_proj(proj, 3, o.reshape(batch * seq, d), x2, b_w_out[0].astype(bf16))
    return x2.reshape(batch, seq, d)
```

```python
import functools
import math

import jax
import jax.numpy as jnp
import numpy as np
from jax import lax
from jax.experimental import pallas as pl
from jax.experimental.pallas import tpu as pltpu

D_MODEL = 1024
HEAD_DIM = 64
LANES = 128
ROPE_THETA = 10000.0
EPS = 1e-6
DEPTH = 2
N_A_LAYERS = DEPTH // 2
VMEM_LIMIT_BYTES = 56 * 1024 * 1024

PROJ_ROWS = 512
OUT_PROJ_ROWS = 1024
PROJ_COLS = 1024
ATT_BLOCK = 256
SB_GROUP = 4
SB_HEADS = 8
DIFF_BLOCK = 512
ONES_ROWS = 16
NEG_BIG = -1e30
LOG2E = math.log2(math.e)
DIFF_Q_SCALE = LOG2E / math.sqrt(HEAD_DIM)
SB_Q_SCALE = -LOG2E / math.sqrt(HEAD_DIM)
UNDERFLOW_LOG2 = -150.0

_NT = (((1,), (1,)), ((), ()))


def _params(*sem):
    return pltpu.CompilerParams(dimension_semantics=sem, vmem_limit_bytes=VMEM_LIMIT_BYTES)


def _rope_kernel(pos_ref, freq_ref, cos_ref, sin_ref):
    ang = pos_ref[...].astype(jnp.float32) * freq_ref[...]
    lane = lax.broadcasted_iota(jnp.int32, ang.shape, 1)
    first_half = (lane % HEAD_DIM) < (HEAD_DIM // 2)
    cos_ref[...] = jnp.cos(ang)
    sin = jnp.sin(ang)
    sin_ref[...] = jnp.where(first_half, -sin, sin)


def _rope_tables(positions):
    n = positions.size
    rows = 1024
    inv_freq = ROPE_THETA ** (-jnp.arange(0, HEAD_DIM, 2, dtype=jnp.float32) / HEAD_DIM)
    freq = jnp.tile(inv_freq, LANES // (HEAD_DIM // 2)).reshape(1, LANES)
    out = jax.ShapeDtypeStruct((n, LANES), jnp.float32)
    return pl.pallas_call(
        _rope_kernel,
        grid=(n // rows,),
        in_specs=[pl.BlockSpec((rows, 1), lambda i: (i, 0)),
                  pl.BlockSpec((1, LANES), lambda i: (0, 0))],
        out_specs=[pl.BlockSpec((rows, LANES), lambda i: (i, 0))] * 2,
        out_shape=[out, out],
        compiler_params=_params("parallel"),
        name="rope_tables",
    )(positions.reshape(n, 1), freq)


def _rms_scale(x):
    return x * lax.rsqrt(jnp.mean(x * x, axis=-1, keepdims=True) + EPS)


def _proj_a_kernel(x_ref, g_ref, w_ref, o_ref):
    h = (_rms_scale(x_ref[...]) * g_ref[...]).astype(jnp.bfloat16)
    for c in range(o_ref.shape[1] // PROJ_COLS):
        cols = slice(c * PROJ_COLS, (c + 1) * PROJ_COLS)
        acc = jnp.dot(h, w_ref[:, cols], preferred_element_type=jnp.float32)
        if c == 0:
            acc = acc * SB_Q_SCALE
        o_ref[:, cols] = acc.astype(o_ref.dtype)


def _proj_a(x2, g, w):
    m, d = x2.shape
    n = w.shape[1]
    return pl.pallas_call(
        _proj_a_kernel,
        grid=(m // PROJ_ROWS,),
        in_specs=[pl.BlockSpec((PROJ_ROWS, d), lambda i: (i, 0)),
                  pl.BlockSpec((1, d), lambda i: (0, 0)),
                  pl.BlockSpec((d, n), lambda i: (0, 0))],
        out_specs=pl.BlockSpec((PROJ_ROWS, n), lambda i: (i, 0)),
        out_shape=jax.ShapeDtypeStruct((m, n), jnp.bfloat16),
        compiler_params=_params("parallel"),
        name="proj_a",
    )(x2, g.reshape(1, d), w)


def _split_bf16(x):
    hi = x.astype(jnp.bfloat16)
    lo = (x - hi.astype(jnp.float32)).astype(jnp.bfloat16)
    return hi, lo


def _head_norm_rope(acc, ng, cos, sin, seg_mean):
    width = seg_mean.shape[0]
    lane = lax.broadcasted_iota(jnp.int32, cos.shape, 1)
    first_half = (lane % HEAD_DIM) < (HEAD_DIM // 2)
    outs = []
    for c in range(acc.shape[1] // width):
        t = acc[:, c * width:(c + 1) * width]
        hi, lo = _split_bf16(t * t)
        ms = (jnp.dot(hi, seg_mean, preferred_element_type=jnp.float32)
              + jnp.dot(lo, seg_mean, preferred_element_type=jnp.float32))
        tn = t * lax.rsqrt(ms + EPS) * ng[:, c * width:(c + 1) * width]
        for s in range(width // LANES):
            u = tn[:, s * LANES:(s + 1) * LANES]
            swapped = jnp.where(first_half,
                                pltpu.roll(u, LANES - HEAD_DIM // 2, axis=1),
                                pltpu.roll(u, HEAD_DIM // 2, axis=1))
            outs.append(u * cos + swapped * sin)
    return jnp.concatenate(outs, axis=1)


def _proj_b_kernel(x_ref, g_ref, w_ref, ng_ref, cos_ref, sin_ref, seg_ref, o_ref, vt_ref):
    xn = _rms_scale(x_ref[...])
    for c in range(o_ref.shape[1] // PROJ_COLS):
        cols = slice(c * PROJ_COLS, (c + 1) * PROJ_COLS)
        h = (xn * g_ref[c // 2]).astype(jnp.bfloat16)
        acc = jnp.dot(h, w_ref[:, cols], preferred_element_type=jnp.float32)
        if c % 2 == 0:
            acc = _head_norm_rope(acc, ng_ref[c // 2], cos_ref[...], sin_ref[...], seg_ref[...])
        if c == 1:
            vt_ref[0] = acc.T.astype(vt_ref.dtype)
        if c == 2:
            acc = acc * DIFF_Q_SCALE
        o_ref[:, cols] = acc.astype(o_ref.dtype)


def _proj_b(x2, g2, w, ng2, cos, sin, batch):
    m, d = x2.shape
    n = w.shape[1]
    seq = m // batch
    tiles = seq // PROJ_ROWS
    width = 2 * LANES
    seg = np.arange(width)[:, None] // HEAD_DIM == np.arange(width)[None, :] // HEAD_DIM
    seg_mean = jnp.asarray(seg.astype(np.float32) / HEAD_DIM, jnp.bfloat16)
    return pl.pallas_call(
        _proj_b_kernel,
        grid=(m // PROJ_ROWS,),
        in_specs=[pl.BlockSpec((PROJ_ROWS, d), lambda i: (i, 0)),
                  pl.BlockSpec((2, 1, d), lambda i: (0, 0, 0)),
                  pl.BlockSpec((d, n), lambda i: (0, 0)),
                  pl.BlockSpec((2, 1, PROJ_COLS), lambda i: (0, 0, 0)),
                  pl.BlockSpec((PROJ_ROWS, LANES), lambda i: (i, 0)),
                  pl.BlockSpec((PROJ_ROWS, LANES), lambda i: (i, 0)),
                  pl.BlockSpec((width, width), lambda i: (0, 0))],
        out_specs=[pl.BlockSpec((PROJ_ROWS, n), lambda i: (i, 0)),
                   pl.BlockSpec((1, d, PROJ_ROWS), lambda i: (i // tiles, 0, i % tiles))],
        out_shape=[jax.ShapeDtypeStruct((m, n), jnp.bfloat16),
                   jax.ShapeDtypeStruct((batch, d, seq), jnp.bfloat16)],
        compiler_params=_params("parallel"),
        name="proj_b",
    )(x2, g2, w, ng2, cos, sin, seg_mean)


def _out_proj_kernel(gate_ref, o_ref, x_ref, w_ref, y_ref):
    gate = gate_ref[...].astype(jnp.float32)
    u = (gate * jax.nn.sigmoid(gate) * o_ref[...].astype(jnp.float32)).astype(jnp.bfloat16)
    y_ref[...] = x_ref[...] + jnp.dot(u, w_ref[...], preferred_element_type=jnp.float32)


def _out_proj(proj, gate_col_block, o, x2, w):
    m, d = x2.shape
    return pl.pallas_call(
        _out_proj_kernel,
        grid=(m // OUT_PROJ_ROWS,),
        in_specs=[pl.BlockSpec((OUT_PROJ_ROWS, d), lambda i: (i, gate_col_block)),
                  pl.BlockSpec((OUT_PROJ_ROWS, d), lambda i: (i, 0)),
                  pl.BlockSpec((OUT_PROJ_ROWS, d), lambda i: (i, 0)),
                  pl.BlockSpec((d, d), lambda i: (0, 0))],
        out_specs=pl.BlockSpec((OUT_PROJ_ROWS, d), lambda i: (i, 0)),
        out_shape=jax.ShapeDtypeStruct((m, d), jnp.float32),
        compiler_params=_params("parallel"),
        name="out_proj",
    )(proj, o, x2, w)


def _sb_kernel(q_ref, k_ref, v_ref, tri_ref, o_ref, acc_ref, c_ref, nz_ref, a_ref):
    tq = ATT_BLOCK
    i = pl.program_id(2)
    heads = range(SB_HEADS)
    gw = SB_GROUP * HEAD_DIM
    lane = lax.broadcasted_iota(jnp.int32, (1, gw), 1)

    def group(x, h):
        g = h // SB_GROUP
        return x[:, g * gw:(g + 1) * gw]

    qn = q_ref[0]
    qh = [jnp.where((lane // HEAD_DIM) == h % SB_GROUP, group(qn, h), 0).astype(qn.dtype)
          for h in heads]
    tri2 = tri_ref[...]

    def tile(ref, j):
        return ref[0, pl.ds(pl.multiple_of(j * tq, tq), tq), :]

    def scores(j):
        kb = tile(k_ref, j)
        return [lax.dot_general(qh[h], group(kb, h), _NT, preferred_element_type=jnp.float32)
                for h in heads]

    def weights(n, carry, strict):
        l = jnp.minimum(n, 0.0) - jnp.log2(1.0 + jnp.exp2(-jnp.abs(n)))
        if strict is not None:
            l = jnp.where(strict, l, 0.0)
        hi, lo = _split_bf16(l)
        incl = jnp.dot(jnp.concatenate([hi, lo], axis=1), tri2,
                       preferred_element_type=jnp.float32)
        a = jnp.exp2(incl - n + carry)
        if strict is not None:
            a = jnp.where(strict, a, 0.0)
        return a.astype(jnp.bfloat16), carry + incl[:, :1]

    def step(j, strict, first):
        n_next = scores(jnp.maximum(j - 1, 0))
        n = [nz_ref[h] for h in heads]
        if not first:
            vb = tile(v_ref, j + 1)
            for h in heads:
                acc_ref[h] += jnp.dot(a_ref[h], group(vb, h),
                                      preferred_element_type=jnp.float32)
        carries = []
        for h in heads:
            carry = jnp.zeros((tq, 1), jnp.float32) if first else c_ref[h]
            a_ref[h], carry = weights(n[h], carry, strict)
            c_ref[h] = carry
            carries.append(carry)
            nz_ref[h] = n_next[h]
        return jnp.max(functools.reduce(jnp.maximum, carries))

    n0 = scores(i)
    for h in heads:
        nz_ref[h] = n0[h]
    acc_ref[...] = jnp.zeros_like(acc_ref)
    row = lax.broadcasted_iota(jnp.int32, (tq, tq), 0)
    col = lax.broadcasted_iota(jnp.int32, (tq, tq), 1)
    stick = step(i, col < row, True)

    def cond(state):
        t, stick = state
        return jnp.logical_and(t < i, stick > UNDERFLOW_LOG2)

    def body(state):
        t, _ = state
        return t + 1, step(i - 1 - t, None, False)

    done, _ = lax.while_loop(cond, body, (jnp.int32(0), stick))
    vb = tile(v_ref, i - done)
    out = [acc_ref[h] + jnp.dot(a_ref[h], group(vb, h), preferred_element_type=jnp.float32)
           for h in heads]
    res = []
    for g in range(SB_HEADS // SB_GROUP):
        r = out[g * SB_GROUP]
        for h in range(1, SB_GROUP):
            r = jnp.where((lane // HEAD_DIM) == h, out[g * SB_GROUP + h], r)
        res.append(r)
    o_ref[0] = jnp.concatenate(res, axis=1).astype(o_ref.dtype)


def _stick_breaking(qkvg, batch, seq):
    tq = ATT_BLOCK
    width = SB_HEADS * HEAD_DIM
    blocks = D_MODEL // width
    tri = (np.arange(tq)[:, None] >= np.arange(tq)[None, :]).astype(np.float32)
    tri2 = jnp.asarray(np.concatenate([tri, tri], axis=0), jnp.bfloat16)
    return pl.pallas_call(
        _sb_kernel,
        grid=(batch, blocks, seq // tq),
        in_specs=[pl.BlockSpec((1, tq, width), lambda b, h, i: (b, i, h)),
                  pl.BlockSpec((1, seq, width), lambda b, h, i: (b, 0, blocks + h)),
                  pl.BlockSpec((1, seq, width), lambda b, h, i: (b, 0, 2 * blocks + h)),
                  pl.BlockSpec((2 * tq, tq), lambda b, h, i: (0, 0))],
        out_specs=pl.BlockSpec((1, tq, width), lambda b, h, i: (b, i, h)),
        out_shape=jax.ShapeDtypeStruct((batch, seq, D_MODEL), jnp.bfloat16),
        scratch_shapes=[pltpu.VMEM((SB_HEADS, tq, SB_GROUP * HEAD_DIM), jnp.float32),
                        pltpu.VMEM((SB_HEADS, tq, 1), jnp.float32),
                        pltpu.VMEM((SB_HEADS, tq, tq), jnp.float32),
                        pltpu.VMEM((SB_HEADS, tq, tq), jnp.bfloat16)],
        compiler_params=_params("parallel", "parallel", "arbitrary"),
        name="stick_breaking",
    )(qkvg, qkvg, qkvg, tri2)


def _diff_kernel(q_ref, k_ref, vt_ref, lam_ref, sg_ref, o_ref, acc_ref, m_ref, alpha_ref, s_ref,
                 p_ref, *, lam_init):
    tq = tk = DIFF_BLOCK
    i = pl.program_id(2)
    comps = range(2)
    lane = lax.broadcasted_iota(jnp.int32, (1, LANES), 1)
    qs = q_ref[0]
    qc = [jnp.where((lane // HEAD_DIM) == c, qs, jnp.zeros_like(qs)) for c in comps]
    ones = jnp.ones((ONES_ROWS, tk), jnp.bfloat16)

    def scores(j):
        kb = k_ref[0, pl.ds(pl.multiple_of(j * tk, tk), tk), :]
        return [lax.dot_general(kb, qc[c], _NT, preferred_element_type=jnp.float32)
                for c in comps]

    def accumulate(acc, alpha, p, j):
        vt = vt_ref[0, :, pl.ds(pl.multiple_of(j * tk, tk), tk)]
        vt1 = jnp.concatenate([vt, ones], axis=0)
        return [alpha[c] * acc[c] + jnp.dot(vt1, p[c], preferred_element_type=jnp.float32)
                for c in comps]

    def softmax_tile(s, m, causal):
        p, alpha, m_out = [], [], []
        for c in comps:
            sc = s[c] if causal is None else jnp.where(causal, s[c], NEG_BIG)
            m_new = jnp.maximum(m[c], jnp.max(sc, axis=0, keepdims=True))
            alpha.append(jnp.exp2(m[c] - m_new))
            p.append(jnp.exp2(sc - m_new).astype(jnp.bfloat16))
            m_out.append(m_new)
        return p, alpha, m_out

    def load(ref):
        return [ref[c] for c in comps]

    def store(ref, vals):
        for c in comps:
            ref[c] = vals[c]

    def step(j):
        s_next = scores(jnp.maximum(j - 1, 0))
        acc = accumulate(load(acc_ref), load(alpha_ref), load(p_ref), j + 1)
        p, alpha, m = softmax_tile(load(s_ref), load(m_ref), None)
        store(acc_ref, acc), store(p_ref, p), store(alpha_ref, alpha), store(m_ref, m)
        store(s_ref, s_next)

    key = lax.broadcasted_iota(jnp.int32, (tk, tq), 0)
    qry = lax.broadcasted_iota(jnp.int32, (tk, tq), 1)
    s_diag = scores(i)
    s_next = scores(jnp.maximum(i - 1, 0))
    m0 = [jnp.full((1, tq), NEG_BIG, jnp.float32) for c in comps]
    p, alpha, m = softmax_tile(s_diag, m0, key <= qry)
    store(p_ref, p), store(alpha_ref, alpha), store(m_ref, m), store(s_ref, s_next)
    acc_ref[...] = jnp.zeros_like(acc_ref)

    def body(t, _):
        step(i - 1 - t)
        return 0

    lax.fori_loop(0, i, body, 0)
    acc = accumulate(load(acc_ref), load(alpha_ref), load(p_ref), 0)

    lp = lam_ref[...]
    lam = (jnp.exp(jnp.sum(lp[0:1] * lp[1:2], axis=-1, keepdims=True))
           - jnp.exp(jnp.sum(lp[2:3] * lp[3:4], axis=-1, keepdims=True)) + lam_init)
    num = [acc[c][:LANES] for c in comps]
    den = [acc[c][LANES:LANES + 1] for c in comps]
    ot = num[0] / den[0] - lam * (num[1] / den[1])
    ot = ot * lax.rsqrt(jnp.mean(ot * ot, axis=0, keepdims=True) + EPS)
    o = ot.T * sg_ref[...] * (1.0 - lam_init)
    o_ref[0] = o.astype(o_ref.dtype)


def _diff_attention(proj, vt, lam_params, subln_g, lam_init, batch, seq):
    tq = DIFF_BLOCK
    heads = D_MODEL // LANES
    rows = LANES + ONES_ROWS
    return pl.pallas_call(
        functools.partial(_diff_kernel, lam_init=lam_init),
        grid=(batch, heads, seq // tq),
        in_specs=[pl.BlockSpec((1, tq, LANES), lambda b, h, i: (b, i, 2 * heads + h)),
                  pl.BlockSpec((1, seq, LANES), lambda b, h, i: (b, 0, h)),
                  pl.BlockSpec((1, LANES, seq), lambda b, h, i: (b, h, 0)),
                  pl.BlockSpec((4, HEAD_DIM), lambda b, h, i: (0, 0)),
                  pl.BlockSpec((1, LANES), lambda b, h, i: (0, 0))],
        out_specs=pl.BlockSpec((1, tq, LANES), lambda b, h, i: (b, i, h)),
        out_shape=jax.ShapeDtypeStruct((batch, seq, D_MODEL), jnp.bfloat16),
        scratch_shapes=[pltpu.VMEM((2, rows, tq), jnp.float32),
                        pltpu.VMEM((2, 1, tq), jnp.float32),
                        pltpu.VMEM((2, 1, tq), jnp.float32),
                        pltpu.VMEM((2, tq, tq), jnp.float32),
                        pltpu.VMEM((2, tq, tq), jnp.bfloat16)],
        compiler_params=_params("parallel", "parallel", "arbitrary"),
        name="diff_attention",
    )(proj, proj, vt, lam_params, subln_g.reshape(1, LANES))


def kernel(x, positions, a_norm_g, a_w_in, a_w_out, kv_norm_g, w_kv, k_norm_g, b_norm_g,
           b_w_in, b_q_norm_g, b_lambda, b_subln_g, b_w_out):
    batch, seq, d = x.shape
    assert d == D_MODEL and seq % ATT_BLOCK == 0 and seq % DIFF_BLOCK == 0
    assert seq % PROJ_ROWS == 0 and (batch * seq) % OUT_PROJ_ROWS == 0
    assert a_norm_g.shape[0] == 1 and b_norm_g.shape[0] == 1
    bf16 = jnp.bfloat16
    x2 = x.reshape(batch * seq, d)

    qkvg = _proj_a(x2, a_norm_g[0], a_w_in[0].astype(bf16))
    o = _stick_breaking(qkvg.reshape(batch, seq, 4 * d), batch, seq)
    x2 = _out_proj(qkvg, 3, o.reshape(batch * seq, d), x2, a_w_out[0].astype(bf16))

    cos, sin = _rope_tables(positions)
    w_b = jnp.concatenate([w_kv, b_w_in[0]], axis=1).astype(bf16)
    g2 = jnp.stack([kv_norm_g, b_norm_g[0]]).reshape(2, 1, d)
    ng2 = jnp.stack([jnp.tile(k_norm_g, d // HEAD_DIM),
                     jnp.tile(b_q_norm_g[0], d // HEAD_DIM)]).reshape(2, 1, d)
    proj, vt = _proj_b(x2, g2, w_b, ng2, cos, sin, batch)
    layer_idx = N_A_LAYERS
    lam_init = 0.8 - 0.6 * math.exp(-0.3 * layer_idx)
    o = _diff_attention(proj.reshape(batch, seq, 4 * d), vt, b_lambda[0], b_subln_g[0],
                        lam_init, batch, seq)
    x2 = _out_proj(proj, 3, o.reshape(batch * seq, d), x2, b_w_out[0].astype(bf16))
    return x2.reshape(batch, seq, d)
```

```python
import functools
import math

import jax
import jax.numpy as jnp
import numpy as np
from jax import lax
from jax.experimental import pallas as pl
from jax.experimental.pallas import tpu as pltpu

D_MODEL = 1024
HEAD_DIM = 64
LANES = 128
ROPE_THETA = 10000.0
EPS = 1e-6
DEPTH = 2
N_A_LAYERS = DEPTH // 2
VMEM_LIMIT_BYTES = 56 * 1024 * 1024

PROJ_ROWS = 512
OUT_PROJ_ROWS = 1024
PROJ_COLS = 1024
ATT_BLOCK = 256
SB_GROUP = 4
SB_HEADS = 8
DIFF_BLOCK = 512
ONES_ROWS = 16
NEG_BIG = -1e30
LOG2E = math.log2(math.e)
DIFF_Q_SCALE = LOG2E / math.sqrt(HEAD_DIM)
SB_Q_SCALE = -LOG2E / math.sqrt(HEAD_DIM)
UNDERFLOW_LOG2 = -150.0

_NT = (((1,), (1,)), ((), ()))


def _params(*sem):
    return pltpu.CompilerParams(dimension_semantics=sem, vmem_limit_bytes=VMEM_LIMIT_BYTES)


def _rope_kernel(pos_ref, freq_ref, cos_ref, sin_ref):
    ang = pos_ref[...].astype(jnp.float32) * freq_ref[...]
    lane = lax.broadcasted_iota(jnp.int32, ang.shape, 1)
    first_half = (lane % HEAD_DIM) < (HEAD_DIM // 2)
    cos_ref[...] = jnp.cos(ang)
    sin = jnp.sin(ang)
    sin_ref[...] = jnp.where(first_half, -sin, sin)


def _rope_tables(positions):
    n = positions.size
    rows = 1024
    inv_freq = ROPE_THETA ** (-jnp.arange(0, HEAD_DIM, 2, dtype=jnp.float32) / HEAD_DIM)
    freq = jnp.tile(inv_freq, LANES // (HEAD_DIM // 2)).reshape(1, LANES)
    out = jax.ShapeDtypeStruct((n, LANES), jnp.float32)
    return pl.pallas_call(
        _rope_kernel,
        grid=(n // rows,),
        in_specs=[pl.BlockSpec((rows, 1), lambda i: (i, 0)),
                  pl.BlockSpec((1, LANES), lambda i: (0, 0))],
        out_specs=[pl.BlockSpec((rows, LANES), lambda i: (i, 0))] * 2,
        out_shape=[out, out],
        compiler_params=_params("parallel"),
        name="rope_tables",
    )(positions.reshape(n, 1), freq)


def _rms_scale(x):
    return x * lax.rsqrt(jnp.mean(x * x, axis=-1, keepdims=True) + EPS)


def _proj_a_kernel(x_ref, g_ref, w_ref, o_ref):
    h = (_rms_scale(x_ref[...]) * g_ref[...]).astype(jnp.bfloat16)
    for c in range(o_ref.shape[1] // PROJ_COLS):
        cols = slice(c * PROJ_COLS, (c + 1) * PROJ_COLS)
        acc = jnp.dot(h, w_ref[:, cols], preferred_element_type=jnp.float32)
        if c == 0:
            acc = acc * SB_Q_SCALE
        o_ref[:, cols] = acc.astype(o_ref.dtype)


def _proj_a(x2, g, w):
    m, d = x2.shape
    n = w.shape[1]
    return pl.pallas_call(
        _proj_a_kernel,
        grid=(m // PROJ_ROWS,),
        in_specs=[pl.BlockSpec((PROJ_ROWS, d), lambda i: (i, 0)),
                  pl.BlockSpec((1, d), lambda i: (0, 0)),
                  pl.BlockSpec((d, n), lambda i: (0, 0))],
        out_specs=pl.BlockSpec((PROJ_ROWS, n), lambda i: (i, 0)),
        out_shape=jax.ShapeDtypeStruct((m, n), jnp.bfloat16),
        compiler_params=_params("parallel"),
        name="proj_a",
    )(x2, g.reshape(1, d), w)


def _split_bf16(x):
    hi = x.astype(jnp.bfloat16)
    lo = (x - hi.astype(jnp.float32)).astype(jnp.bfloat16)
    return hi, lo


def _head_norm_rope(acc, ng, cos, sin, seg_mean):
    width = seg_mean.shape[0]
    lane = lax.broadcasted_iota(jnp.int32, cos.shape, 1)
    first_half = (lane % HEAD_DIM) < (HEAD_DIM // 2)
    outs = []
    for c in range(acc.shape[1] // width):
        t = acc[:, c * width:(c + 1) * width]
        hi, lo = _split_bf16(t * t)
        ms = (jnp.dot(hi, seg_mean, preferred_element_type=jnp.float32)
              + jnp.dot(lo, seg_mean, preferred_element_type=jnp.float32))
        tn = t * lax.rsqrt(ms + EPS) * ng[:, c * width:(c + 1) * width]
        for s in range(width // LANES):
            u = tn[:, s * LANES:(s + 1) * LANES]
            swapped = jnp.where(first_half,
                                pltpu.roll(u, LANES - HEAD_DIM // 2, axis=1),
                                pltpu.roll(u, HEAD_DIM // 2, axis=1))
            outs.append(u * cos + swapped * sin)
    return jnp.concatenate(outs, axis=1)


def _gated_out_proj(gate_ref, attn_ref, x_ref, w_ref):
    gate = gate_ref[...].astype(jnp.float32)
    u = (gate * jax.nn.sigmoid(gate) * attn_ref[...].astype(jnp.float32)).astype(jnp.bfloat16)
    return x_ref[...] + jnp.dot(u, w_ref[...], preferred_element_type=jnp.float32)


def _proj_b_kernel(gate_ref, attn_ref, x_ref, wo_ref, g_ref, w_ref, ng_ref, cos_ref, sin_ref,
                   seg_ref, x1_ref, o_ref, vt_ref):
    x1 = _gated_out_proj(gate_ref, attn_ref, x_ref, wo_ref)
    x1_ref[...] = x1
    xn = _rms_scale(x1)
    for c in range(o_ref.shape[1] // PROJ_COLS):
        cols = slice(c * PROJ_COLS, (c + 1) * PROJ_COLS)
        h = (xn * g_ref[c // 2]).astype(jnp.bfloat16)
        acc = jnp.dot(h, w_ref[:, cols], preferred_element_type=jnp.float32)
        if c % 2 == 0:
            acc = _head_norm_rope(acc, ng_ref[c // 2], cos_ref[...], sin_ref[...], seg_ref[...])
        if c == 1:
            vt_ref[0] = acc.T.astype(vt_ref.dtype)
        if c == 2:
            acc = acc * DIFF_Q_SCALE
        o_ref[:, cols] = acc.astype(o_ref.dtype)


def _proj_b(prev_proj, prev_attn, x2, w_out, g2, w, ng2, cos, sin, batch):
    m, d = x2.shape
    n = w.shape[1]
    seq = m // batch
    tiles = seq // PROJ_ROWS
    width = 2 * LANES
    seg = np.arange(width)[:, None] // HEAD_DIM == np.arange(width)[None, :] // HEAD_DIM
    seg_mean = jnp.asarray(seg.astype(np.float32) / HEAD_DIM, jnp.bfloat16)
    return pl.pallas_call(
        _proj_b_kernel,
        grid=(m // PROJ_ROWS,),
        in_specs=[pl.BlockSpec((PROJ_ROWS, d), lambda i: (i, 3)),
                  pl.BlockSpec((PROJ_ROWS, d), lambda i: (i, 0)),
                  pl.BlockSpec((PROJ_ROWS, d), lambda i: (i, 0)),
                  pl.BlockSpec((d, d), lambda i: (0, 0)),
                  pl.BlockSpec((2, 1, d), lambda i: (0, 0, 0)),
                  pl.BlockSpec((d, n), lambda i: (0, 0)),
                  pl.BlockSpec((2, 1, PROJ_COLS), lambda i: (0, 0, 0)),
                  pl.BlockSpec((PROJ_ROWS, LANES), lambda i: (i, 0)),
                  pl.BlockSpec((PROJ_ROWS, LANES), lambda i: (i, 0)),
                  pl.BlockSpec((width, width), lambda i: (0, 0))],
        out_specs=[pl.BlockSpec((PROJ_ROWS, d), lambda i: (i, 0)),
                   pl.BlockSpec((PROJ_ROWS, n), lambda i: (i, 0)),
                   pl.BlockSpec((1, d, PROJ_ROWS), lambda i: (i // tiles, 0, i % tiles))],
        out_shape=[jax.ShapeDtypeStruct((m, d), jnp.float32),
                   jax.ShapeDtypeStruct((m, n), jnp.bfloat16),
                   jax.ShapeDtypeStruct((batch, d, seq), jnp.bfloat16)],
        compiler_params=_params("parallel"),
        name="proj_b",
    )(prev_proj, prev_attn, x2, w_out, g2, w, ng2, cos, sin, seg_mean)


def _out_proj_kernel(gate_ref, attn_ref, x_ref, w_ref, y_ref):
    y_ref[...] = _gated_out_proj(gate_ref, attn_ref, x_ref, w_ref)


def _out_proj(proj, gate_col_block, o, x2, w):
    m, d = x2.shape
    return pl.pallas_call(
        _out_proj_kernel,
        grid=(m // OUT_PROJ_ROWS,),
        in_specs=[pl.BlockSpec((OUT_PROJ_ROWS, d), lambda i: (i, gate_col_block)),
                  pl.BlockSpec((OUT_PROJ_ROWS, d), lambda i: (i, 0)),
                  pl.BlockSpec((OUT_PROJ_ROWS, d), lambda i: (i, 0)),
                  pl.BlockSpec((d, d), lambda i: (0, 0))],
        out_specs=pl.BlockSpec((OUT_PROJ_ROWS, d), lambda i: (i, 0)),
        out_shape=jax.ShapeDtypeStruct((m, d), jnp.float32),
        compiler_params=_params("parallel"),
        name="out_proj",
    )(proj, o, x2, w)


def _sb_kernel(q_ref, k_ref, v_ref, tri_ref, o_ref, acc_ref, c_ref, nz_ref, a_ref):
    tq = ATT_BLOCK
    i = pl.program_id(2)
    heads = range(SB_HEADS)
    gw = SB_GROUP * HEAD_DIM
    lane = lax.broadcasted_iota(jnp.int32, (1, gw), 1)

    def group(x, h):
        g = h // SB_GROUP
        return x[:, g * gw:(g + 1) * gw]

    qn = q_ref[0]
    qh = [jnp.where((lane // HEAD_DIM) == h % SB_GROUP, group(qn, h), 0).astype(qn.dtype)
          for h in heads]
    tri2 = tri_ref[...]

    def tile(ref, j):
        return ref[0, pl.ds(pl.multiple_of(j * tq, tq), tq), :]

    def scores(j):
        kb = tile(k_ref, j)
        return [lax.dot_general(qh[h], group(kb, h), _NT, preferred_element_type=jnp.float32)
                for h in heads]

    def weights(n, carry, strict):
        l = jnp.minimum(n, 0.0) - jnp.log2(1.0 + jnp.exp2(-jnp.abs(n)))
        if strict is not None:
            l = jnp.where(strict, l, 0.0)
        hi, lo = _split_bf16(l)
        incl = jnp.dot(jnp.concatenate([hi, lo], axis=1), tri2,
                       preferred_element_type=jnp.float32)
        a = jnp.exp2(incl - n + carry)
        if strict is not None:
            a = jnp.where(strict, a, 0.0)
        return a.astype(jnp.bfloat16), carry + incl[:, :1]

    def step(j, strict, first):
        n_next = scores(jnp.maximum(j - 1, 0))
        n = [nz_ref[h] for h in heads]
        if not first:
            vb = tile(v_ref, j + 1)
            for h in heads:
                acc_ref[h] += jnp.dot(a_ref[h], group(vb, h),
                                      preferred_element_type=jnp.float32)
        carries = []
        for h in heads:
            carry = jnp.zeros((tq, 1), jnp.float32) if first else c_ref[h]
            a_ref[h], carry = weights(n[h], carry, strict)
            c_ref[h] = carry
            carries.append(carry)
            nz_ref[h] = n_next[h]
        return jnp.max(functools.reduce(jnp.maximum, carries))

    n0 = scores(i)
    for h in heads:
        nz_ref[h] = n0[h]
    acc_ref[...] = jnp.zeros_like(acc_ref)
    row = lax.broadcasted_iota(jnp.int32, (tq, tq), 0)
    col = lax.broadcasted_iota(jnp.int32, (tq, tq), 1)
    stick = step(i, col < row, True)

    def cond(state):
        t, stick = state
        return jnp.logical_and(t < i, stick > UNDERFLOW_LOG2)

    def body(state):
        t, _ = state
        return t + 1, step(i - 1 - t, None, False)

    done, _ = lax.while_loop(cond, body, (jnp.int32(0), stick))
    vb = tile(v_ref, i - done)
    out = [acc_ref[h] + jnp.dot(a_ref[h], group(vb, h), preferred_element_type=jnp.float32)
           for h in heads]
    res = []
    for g in range(SB_HEADS // SB_GROUP):
        r = out[g * SB_GROUP]
        for h in range(1, SB_GROUP):
            r = jnp.where((lane // HEAD_DIM) == h, out[g * SB_GROUP + h], r)
        res.append(r)
    o_ref[0] = jnp.concatenate(res, axis=1).astype(o_ref.dtype)


def _stick_breaking(qkvg, batch, seq):
    tq = ATT_BLOCK
    width = SB_HEADS * HEAD_DIM
    blocks = D_MODEL // width
    tri = (np.arange(tq)[:, None] >= np.arange(tq)[None, :]).astype(np.float32)
    tri2 = jnp.asarray(np.concatenate([tri, tri], axis=0), jnp.bfloat16)
    return pl.pallas_call(
        _sb_kernel,
        grid=(batch, blocks, seq // tq),
        in_specs=[pl.BlockSpec((1, tq, width), lambda b, h, i: (b, i, h)),
                  pl.BlockSpec((1, seq, width), lambda b, h, i: (b, 0, blocks + h)),
                  pl.BlockSpec((1, seq, width), lambda b, h, i: (b, 0, 2 * blocks + h)),
                  pl.BlockSpec((2 * tq, tq), lambda b, h, i: (0, 0))],
        out_specs=pl.BlockSpec((1, tq, width), lambda b, h, i: (b, i, h)),
        out_shape=jax.ShapeDtypeStruct((batch, seq, D_MODEL), jnp.bfloat16),
        scratch_shapes=[pltpu.VMEM((SB_HEADS, tq, SB_GROUP * HEAD_DIM), jnp.float32),
                        pltpu.VMEM((SB_HEADS, tq, 1), jnp.float32),
                        pltpu.VMEM((SB_HEADS, tq, tq), jnp.float32),
                        pltpu.VMEM((SB_HEADS, tq, tq), jnp.bfloat16)],
        compiler_params=_params("parallel", "parallel", "arbitrary"),
        name="stick_breaking",
    )(qkvg, qkvg, qkvg, tri2)


def _diff_kernel(q_ref, k_ref, vt_ref, lam_ref, sg_ref, o_ref, acc_ref, m_ref, alpha_ref, s_ref,
                 p_ref, *, lam_init):
    tq = tk = DIFF_BLOCK
    i = pl.program_id(2)
    comps = range(2)
    lane = lax.broadcasted_iota(jnp.int32, (1, LANES), 1)
    qs = q_ref[0]
    qc = [jnp.where((lane // HEAD_DIM) == c, qs, jnp.zeros_like(qs)) for c in comps]
    ones = jnp.ones((ONES_ROWS, tk), jnp.bfloat16)

    def scores(j):
        kb = k_ref[0, pl.ds(pl.multiple_of(j * tk, tk), tk), :]
        return [lax.dot_general(kb, qc[c], _NT, preferred_element_type=jnp.float32)
                for c in comps]

    def accumulate(acc, alpha, p, j):
        vt = vt_ref[0, :, pl.ds(pl.multiple_of(j * tk, tk), tk)]
        vt1 = jnp.concatenate([vt, ones], axis=0)
        return [alpha[c] * acc[c] + jnp.dot(vt1, p[c], preferred_element_type=jnp.float32)
                for c in comps]

    def softmax_tile(s, m, causal):
        p, alpha, m_out = [], [], []
        for c in comps:
            sc = s[c] if causal is None else jnp.where(causal, s[c], NEG_BIG)
            m_new = jnp.maximum(m[c], jnp.max(sc, axis=0, keepdims=True))
            alpha.append(jnp.exp2(m[c] - m_new))
            p.append(jnp.exp2(sc - m_new).astype(jnp.bfloat16))
            m_out.append(m_new)
        return p, alpha, m_out

    def load(ref):
        return [ref[c] for c in comps]

    def store(ref, vals):
        for c in comps:
            ref[c] = vals[c]

    def step(j):
        s_next = scores(jnp.maximum(j - 1, 0))
        acc = accumulate(load(acc_ref), load(alpha_ref), load(p_ref), j + 1)
        p, alpha, m = softmax_tile(load(s_ref), load(m_ref), None)
        store(acc_ref, acc), store(p_ref, p), store(alpha_ref, alpha), store(m_ref, m)
        store(s_ref, s_next)

    key = lax.broadcasted_iota(jnp.int32, (tk, tq), 0)
    qry = lax.broadcasted_iota(jnp.int32, (tk, tq), 1)
    s_diag = scores(i)
    s_next = scores(jnp.maximum(i - 1, 0))
    m0 = [jnp.full((1, tq), NEG_BIG, jnp.float32) for c in comps]
    p, alpha, m = softmax_tile(s_diag, m0, key <= qry)
    store(p_ref, p), store(alpha_ref, alpha), store(m_ref, m), store(s_ref, s_next)
    acc_ref[...] = jnp.zeros_like(acc_ref)

    def body(t, _):
        step(i - 1 - t)
        return 0

    lax.fori_loop(0, i, body, 0)
    acc = accumulate(load(acc_ref), load(alpha_ref), load(p_ref), 0)

    lp = lam_ref[...]
    lam = (jnp.exp(jnp.sum(lp[0:1] * lp[1:2], axis=-1, keepdims=True))
           - jnp.exp(jnp.sum(lp[2:3] * lp[3:4], axis=-1, keepdims=True)) + lam_init)
    num = [acc[c][:LANES] for c in comps]
    den = [acc[c][LANES:LANES + 1] for c in comps]
    ot = num[0] / den[0] - lam * (num[1] / den[1])
    ot = ot * lax.rsqrt(jnp.mean(ot * ot, axis=0, keepdims=True) + EPS)
    o = ot.T * sg_ref[...] * (1.0 - lam_init)
    o_ref[0] = o.astype(o_ref.dtype)


def _diff_attention(proj, vt, lam_params, subln_g, lam_init, batch, seq):
    tq = DIFF_BLOCK
    heads = D_MODEL // LANES
    rows = LANES + ONES_ROWS
    return pl.pallas_call(
        functools.partial(_diff_kernel, lam_init=lam_init),
        grid=(batch, heads, seq // tq),
        in_specs=[pl.BlockSpec((1, tq, LANES), lambda b, h, i: (b, i, 2 * heads + h)),
                  pl.BlockSpec((1, seq, LANES), lambda b, h, i: (b, 0, h)),
                  pl.BlockSpec((1, LANES, seq), lambda b, h, i: (b, h, 0)),
                  pl.BlockSpec((4, HEAD_DIM), lambda b, h, i: (0, 0)),
                  pl.BlockSpec((1, LANES), lambda b, h, i: (0, 0))],
        out_specs=pl.BlockSpec((1, tq, LANES), lambda b, h, i: (b, i, h)),
        out_shape=jax.ShapeDtypeStruct((batch, seq, D_MODEL), jnp.bfloat16),
        scratch_shapes=[pltpu.VMEM((2, rows, tq), jnp.float32),
                        pltpu.VMEM((2, 1, tq), jnp.float32),
                        pltpu.VMEM((2, 1, tq), jnp.float32),
                        pltpu.VMEM((2, tq, tq), jnp.float32),
                        pltpu.VMEM((2, tq, tq), jnp.bfloat16)],
        compiler_params=_params("parallel", "parallel", "arbitrary"),
        name="diff_attention",
    )(proj, proj, vt, lam_params, subln_g.reshape(1, LANES))


def kernel(x, positions, a_norm_g, a_w_in, a_w_out, kv_norm_g, w_kv, k_norm_g, b_norm_g,
           b_w_in, b_q_norm_g, b_lambda, b_subln_g, b_w_out):
    batch, seq, d = x.shape
    assert d == D_MODEL and seq % ATT_BLOCK == 0 and seq % DIFF_BLOCK == 0
    assert seq % PROJ_ROWS == 0 and (batch * seq) % OUT_PROJ_ROWS == 0
    assert a_norm_g.shape[0] == 1 and b_norm_g.shape[0] == 1
    bf16 = jnp.bfloat16
    x2 = x.reshape(batch * seq, d)

    qkvg = _proj_a(x2, a_norm_g[0], a_w_in[0].astype(bf16))
    o = _stick_breaking(qkvg.reshape(batch, seq, 4 * d), batch, seq)

    cos, sin = _rope_tables(positions)
    w_b = jnp.concatenate([w_kv, b_w_in[0]], axis=1).astype(bf16)
    g2 = jnp.stack([kv_norm_g, b_norm_g[0]]).reshape(2, 1, d)
    ng2 = jnp.stack([jnp.tile(k_norm_g, d // HEAD_DIM),
                     jnp.tile(b_q_norm_g[0], d // HEAD_DIM)]).reshape(2, 1, d)
    x2, proj, vt = _proj_b(qkvg, o.reshape(batch * seq, d), x2, a_w_out[0].astype(bf16), g2, w_b,
                           ng2, cos, sin, batch)
    layer_idx = N_A_LAYERS
    lam_init = 0.8 - 0.6 * math.exp(-0.3 * layer_idx)
    o = _diff_attention(proj.reshape(batch, seq, 4 * d), vt, b_lambda[0], b_subln_g[0],
                        lam_init, batch, seq)
    x2 = _out_proj(proj, 3, o.reshape(batch * seq, d), x2, b_w_out[0].astype(bf16))
    return x2.reshape(batch, seq, d)
```

```python
import functools
import math

import jax
import jax.numpy as jnp
import numpy as np
from jax import lax
from jax.experimental import pallas as pl
from jax.experimental.pallas import tpu as pltpu

D_MODEL = 1024
HEAD_DIM = 64
LANES = 128
ROPE_THETA = 10000.0
EPS = 1e-6
DEPTH = 2
N_A_LAYERS = DEPTH // 2
VMEM_LIMIT_BYTES = 56 * 1024 * 1024

PROJ_ROWS = 512
OUT_PROJ_ROWS = 1024
PROJ_COLS = 1024
ATT_BLOCK = 256
SB_GROUP = 4
SB_HEADS = 8
DIFF_BLOCK = 512
ONES_ROWS = 16
NEG_BIG = -1e30
LOG2E = math.log2(math.e)
DIFF_Q_SCALE = LOG2E / math.sqrt(HEAD_DIM)
SB_Q_SCALE = -LOG2E / math.sqrt(HEAD_DIM)
UNDERFLOW_LOG2 = -150.0

_NT = (((1,), (1,)), ((), ()))


def _params(*sem):
    return pltpu.CompilerParams(dimension_semantics=sem, vmem_limit_bytes=VMEM_LIMIT_BYTES)


def _rope_kernel(pos_ref, freq_ref, cos_ref, sin_ref):
    ang = pos_ref[...].astype(jnp.float32) * freq_ref[...]
    lane = lax.broadcasted_iota(jnp.int32, ang.shape, 1)
    first_half = (lane % HEAD_DIM) < (HEAD_DIM // 2)
    cos_ref[...] = jnp.cos(ang)
    sin = jnp.sin(ang)
    sin_ref[...] = jnp.where(first_half, -sin, sin)


def _rope_tables(positions):
    n = positions.size
    rows = 1024
    inv_freq = ROPE_THETA ** (-jnp.arange(0, HEAD_DIM, 2, dtype=jnp.float32) / HEAD_DIM)
    freq = jnp.tile(inv_freq, LANES // (HEAD_DIM // 2)).reshape(1, LANES)
    out = jax.ShapeDtypeStruct((n, LANES), jnp.float32)
    return pl.pallas_call(
        _rope_kernel,
        grid=(n // rows,),
        in_specs=[pl.BlockSpec((rows, 1), lambda i: (i, 0)),
                  pl.BlockSpec((1, LANES), lambda i: (0, 0))],
        out_specs=[pl.BlockSpec((rows, LANES), lambda i: (i, 0))] * 2,
        out_shape=[out, out],
        compiler_params=_params("parallel"),
        name="rope_tables",
    )(positions.reshape(n, 1), freq)


def _rms_scale(x):
    return x * lax.rsqrt(jnp.mean(x * x, axis=-1, keepdims=True) + EPS)


def _proj_a_kernel(x_ref, g_ref, w_ref, o_ref):
    h = (_rms_scale(x_ref[...]) * g_ref[...]).astype(jnp.bfloat16)
    for c in range(o_ref.shape[1] // PROJ_COLS):
        cols = slice(c * PROJ_COLS, (c + 1) * PROJ_COLS)
        acc = jnp.dot(h, w_ref[:, cols], preferred_element_type=jnp.float32)
        if c == 0:
            acc = acc * SB_Q_SCALE
        o_ref[:, cols] = acc.astype(o_ref.dtype)


def _proj_a(x2, g, w):
    m, d = x2.shape
    n = w.shape[1]
    return pl.pallas_call(
        _proj_a_kernel,
        grid=(m // PROJ_ROWS,),
        in_specs=[pl.BlockSpec((PROJ_ROWS, d), lambda i: (i, 0)),
                  pl.BlockSpec((1, d), lambda i: (0, 0)),
                  pl.BlockSpec((d, n), lambda i: (0, 0))],
        out_specs=pl.BlockSpec((PROJ_ROWS, n), lambda i: (i, 0)),
        out_shape=jax.ShapeDtypeStruct((m, n), jnp.bfloat16),
        compiler_params=_params("parallel"),
        name="proj_a",
    )(x2, g.reshape(1, d), w)


def _split_bf16(x):
    hi = x.astype(jnp.bfloat16)
    lo = (x - hi.astype(jnp.float32)).astype(jnp.bfloat16)
    return hi, lo


def _head_norm_rope(acc, ng, cos, sin, seg_mean):
    width = seg_mean.shape[0]
    lane = lax.broadcasted_iota(jnp.int32, cos.shape, 1)
    first_half = (lane % HEAD_DIM) < (HEAD_DIM // 2)
    outs = []
    for c in range(acc.shape[1] // width):
        t = acc[:, c * width:(c + 1) * width]
        hi, lo = _split_bf16(t * t)
        ms = (jnp.dot(hi, seg_mean, preferred_element_type=jnp.float32)
              + jnp.dot(lo, seg_mean, preferred_element_type=jnp.float32))
        tn = t * lax.rsqrt(ms + EPS) * ng[:, c * width:(c + 1) * width]
        for s in range(width // LANES):
            u = tn[:, s * LANES:(s + 1) * LANES]
            swapped = jnp.where(first_half,
                                pltpu.roll(u, LANES - HEAD_DIM // 2, axis=1),
                                pltpu.roll(u, HEAD_DIM // 2, axis=1))
            outs.append(u * cos + swapped * sin)
    return jnp.concatenate(outs, axis=1)


def _proj_b_kernel(x_ref, g_ref, w_ref, ng_ref, cos_ref, sin_ref, seg_ref, o_ref, vt_ref):
    xn = _rms_scale(x_ref[...])
    for c in range(o_ref.shape[1] // PROJ_COLS):
        cols = slice(c * PROJ_COLS, (c + 1) * PROJ_COLS)
        h = (xn * g_ref[c // 2]).astype(jnp.bfloat16)
        acc = jnp.dot(h, w_ref[:, cols], preferred_element_type=jnp.float32)
        if c % 2 == 0:
            acc = _head_norm_rope(acc, ng_ref[c // 2], cos_ref[...], sin_ref[...], seg_ref[...])
        if c == 1:
            vt_ref[0] = acc.T.astype(vt_ref.dtype)
        if c == 2:
            acc = acc * DIFF_Q_SCALE
        o_ref[:, cols] = acc.astype(o_ref.dtype)


def _proj_b(x2, g2, w, ng2, cos, sin, batch):
    m, d = x2.shape
    n = w.shape[1]
    seq = m // batch
    tiles = seq // PROJ_ROWS
    width = 2 * LANES
    seg = np.arange(width)[:, None] // HEAD_DIM == np.arange(width)[None, :] // HEAD_DIM
    seg_mean = jnp.asarray(seg.astype(np.float32) / HEAD_DIM, jnp.bfloat16)
    return pl.pallas_call(
        _proj_b_kernel,
        grid=(m // PROJ_ROWS,),
        in_specs=[pl.BlockSpec((PROJ_ROWS, d), lambda i: (i, 0)),
                  pl.BlockSpec((2, 1, d), lambda i: (0, 0, 0)),
                  pl.BlockSpec((d, n), lambda i: (0, 0)),
                  pl.BlockSpec((2, 1, PROJ_COLS), lambda i: (0, 0, 0)),
                  pl.BlockSpec((PROJ_ROWS, LANES), lambda i: (i, 0)),
                  pl.BlockSpec((PROJ_ROWS, LANES), lambda i: (i, 0)),
                  pl.BlockSpec((width, width), lambda i: (0, 0))],
        out_specs=[pl.BlockSpec((PROJ_ROWS, n), lambda i: (i, 0)),
                   pl.BlockSpec((1, d, PROJ_ROWS), lambda i: (i // tiles, 0, i % tiles))],
        out_shape=[jax.ShapeDtypeStruct((m, n), jnp.bfloat16),
                   jax.ShapeDtypeStruct((batch, d, seq), jnp.bfloat16)],
        compiler_params=_params("parallel"),
        name="proj_b",
    )(x2, g2, w, ng2, cos, sin, seg_mean)


def _out_proj_kernel(gate_ref, o_ref, x_ref, w_ref, y_ref):
    gate = gate_ref[...].astype(jnp.float32)
    u = (gate * jax.nn.sigmoid(gate) * o_ref[...].astype(jnp.float32)).astype(jnp.bfloat16)
    y_ref[...] = x_ref[...] + jnp.dot(u, w_ref[...], preferred_element_type=jnp.float32)


def _out_proj(proj, gate_col_block, o, x2, w):
    m, d = x2.shape
    return pl.pallas_call(
        _out_proj_kernel,
        grid=(m // OUT_PROJ_ROWS,),
        in_specs=[pl.BlockSpec((OUT_PROJ_ROWS, d), lambda i: (i, gate_col_block)),
                  pl.BlockSpec((OUT_PROJ_ROWS, d), lambda i: (i, 0)),
                  pl.BlockSpec((OUT_PROJ_ROWS, d), lambda i: (i, 0)),
                  pl.BlockSpec((d, d), lambda i: (0, 0))],
        out_specs=pl.BlockSpec((OUT_PROJ_ROWS, d), lambda i: (i, 0)),
        out_shape=jax.ShapeDtypeStruct((m, d), jnp.float32),
        compiler_params=_params("parallel"),
        name="out_proj",
    )(proj, o, x2, w)


def _sb_kernel(q_ref, k_ref, v_ref, tri_ref, o_ref, acc_ref, c_ref, a_ref):
    tq = ATT_BLOCK
    i = pl.program_id(2)
    heads = range(SB_HEADS)
    gw = SB_GROUP * HEAD_DIM
    lane = lax.broadcasted_iota(jnp.int32, (1, gw), 1)

    def group(x, h):
        g = h // SB_GROUP
        return x[:, g * gw:(g + 1) * gw]

    qn = q_ref[0]
    qh = [jnp.where((lane // HEAD_DIM) == h % SB_GROUP, group(qn, h), 0).astype(qn.dtype)
          for h in heads]
    tri2 = tri_ref[...]

    def tile(ref, j):
        return ref[0, pl.ds(pl.multiple_of(j * tq, tq), tq), :]

    def scores(j):
        kb = tile(k_ref, j)
        return [lax.dot_general(qh[h], group(kb, h), _NT, preferred_element_type=jnp.float32)
                for h in heads]

    def weights(n, carry, strict):
        l = jnp.minimum(n, 0.0) - jnp.log2(1.0 + jnp.exp2(-jnp.abs(n)))
        if strict is not None:
            l = jnp.where(strict, l, 0.0)
        hi, lo = _split_bf16(l)
        incl = jnp.dot(jnp.concatenate([hi, lo], axis=1), tri2,
                       preferred_element_type=jnp.float32)
        a = jnp.exp2(incl - n + carry)
        if strict is not None:
            a = jnp.where(strict, a, 0.0)
        return a.astype(jnp.bfloat16), carry + incl[:, :1]

    def step(j, strict, first):
        n = scores(j)
        if not first:
            vb = tile(v_ref, j + 1)
            for h in heads:
                acc_ref[h] += jnp.dot(a_ref[h], group(vb, h),
                                      preferred_element_type=jnp.float32)
        carries = []
        for h in heads:
            carry = jnp.zeros((tq, 1), jnp.float32) if first else c_ref[h]
            a_ref[h], carry = weights(n[h], carry, strict)
            c_ref[h] = carry
            carries.append(carry)
        return jnp.max(functools.reduce(jnp.maximum, carries))

    acc_ref[...] = jnp.zeros_like(acc_ref)
    row = lax.broadcasted_iota(jnp.int32, (tq, tq), 0)
    col = lax.broadcasted_iota(jnp.int32, (tq, tq), 1)
    stick = step(i, col < row, True)

    def cond(state):
        t, stick = state
        return jnp.logical_and(t < i, stick > UNDERFLOW_LOG2)

    def body(state):
        t, _ = state
        return t + 1, step(i - 1 - t, None, False)

    done, _ = lax.while_loop(cond, body, (jnp.int32(0), stick))
    vb = tile(v_ref, i - done)
    out = [acc_ref[h] + jnp.dot(a_ref[h], group(vb, h), preferred_element_type=jnp.float32)
           for h in heads]
    res = []
    for g in range(SB_HEADS // SB_GROUP):
        r = out[g * SB_GROUP]
        for h in range(1, SB_GROUP):
            r = jnp.where((lane // HEAD_DIM) == h, out[g * SB_GROUP + h], r)
        res.append(r)
    o_ref[0] = jnp.concatenate(res, axis=1).astype(o_ref.dtype)


def _stick_breaking(qkvg, batch, seq):
    tq = ATT_BLOCK
    width = SB_HEADS * HEAD_DIM
    blocks = D_MODEL // width
    tri = (np.arange(tq)[:, None] >= np.arange(tq)[None, :]).astype(np.float32)
    tri2 = jnp.asarray(np.concatenate([tri, tri], axis=0), jnp.bfloat16)
    return pl.pallas_call(
        _sb_kernel,
        grid=(batch, blocks, seq // tq),
        in_specs=[pl.BlockSpec((1, tq, width), lambda b, h, i: (b, i, h)),
                  pl.BlockSpec((1, seq, width), lambda b, h, i: (b, 0, blocks + h)),
                  pl.BlockSpec((1, seq, width), lambda b, h, i: (b, 0, 2 * blocks + h)),
                  pl.BlockSpec((2 * tq, tq), lambda b, h, i: (0, 0))],
        out_specs=pl.BlockSpec((1, tq, width), lambda b, h, i: (b, i, h)),
        out_shape=jax.ShapeDtypeStruct((batch, seq, D_MODEL), jnp.bfloat16),
        scratch_shapes=[pltpu.VMEM((SB_HEADS, tq, SB_GROUP * HEAD_DIM), jnp.float32),
                        pltpu.VMEM((SB_HEADS, tq, 1), jnp.float32),
                        pltpu.VMEM((SB_HEADS, tq, tq), jnp.bfloat16)],
        compiler_params=_params("parallel", "parallel", "arbitrary"),
        name="stick_breaking",
    )(qkvg, qkvg, qkvg, tri2)


def _diff_kernel(q_ref, k_ref, vt_ref, lam_ref, sg_ref, o_ref, acc_ref, m_ref, alpha_ref, s_ref,
                 p_ref, *, lam_init):
    tq = tk = DIFF_BLOCK
    i = pl.program_id(2)
    comps = range(2)
    lane = lax.broadcasted_iota(jnp.int32, (1, LANES), 1)
    qs = q_ref[0]
    qc = [jnp.where((lane // HEAD_DIM) == c, qs, jnp.zeros_like(qs)) for c in comps]
    ones = jnp.ones((ONES_ROWS, tk), jnp.bfloat16)

    def scores(j):
        kb = k_ref[0, pl.ds(pl.multiple_of(j * tk, tk), tk), :]
        return [lax.dot_general(kb, qc[c], _NT, preferred_element_type=jnp.float32)
                for c in comps]

    def accumulate(acc, alpha, p, j):
        vt = vt_ref[0, :, pl.ds(pl.multiple_of(j * tk, tk), tk)]
        vt1 = jnp.concatenate([vt, ones], axis=0)
        return [alpha[c] * acc[c] + jnp.dot(vt1, p[c], preferred_element_type=jnp.float32)
                for c in comps]

    def softmax_tile(s, m, causal):
        p, alpha, m_out = [], [], []
        for c in comps:
            sc = s[c] if causal is None else jnp.where(causal, s[c], NEG_BIG)
            m_new = jnp.maximum(m[c], jnp.max(sc, axis=0, keepdims=True))
            alpha.append(jnp.exp2(m[c] - m_new))
            p.append(jnp.exp2(sc - m_new).astype(jnp.bfloat16))
            m_out.append(m_new)
        return p, alpha, m_out

    def load(ref):
        return [ref[c] for c in comps]

    def store(ref, vals):
        for c in comps:
            ref[c] = vals[c]

    def step(j):
        s_next = scores(jnp.maximum(j - 1, 0))
        acc = accumulate(load(acc_ref), load(alpha_ref), load(p_ref), j + 1)
        p, alpha, m = softmax_tile(load(s_ref), load(m_ref), None)
        store(acc_ref, acc), store(p_ref, p), store(alpha_ref, alpha), store(m_ref, m)
        store(s_ref, s_next)

    key = lax.broadcasted_iota(jnp.int32, (tk, tq), 0)
    qry = lax.broadcasted_iota(jnp.int32, (tk, tq), 1)
    s_diag = scores(i)
    s_next = scores(jnp.maximum(i - 1, 0))
    m0 = [jnp.full((1, tq), NEG_BIG, jnp.float32) for c in comps]
    p, alpha, m = softmax_tile(s_diag, m0, key <= qry)
    store(p_ref, p), store(alpha_ref, alpha), store(m_ref, m), store(s_ref, s_next)
    acc_ref[...] = jnp.zeros_like(acc_ref)

    def body(t, _):
        step(i - 1 - t)
        return 0

    lax.fori_loop(0, i, body, 0)
    acc = accumulate(load(acc_ref), load(alpha_ref), load(p_ref), 0)

    lp = lam_ref[...]
    lam = (jnp.exp(jnp.sum(lp[0:1] * lp[1:2], axis=-1, keepdims=True))
           - jnp.exp(jnp.sum(lp[2:3] * lp[3:4], axis=-1, keepdims=True)) + lam_init)
    num = [acc[c][:LANES] for c in comps]
    den = [acc[c][LANES:LANES + 1] for c in comps]
    ot = num[0] / den[0] - lam * (num[1] / den[1])
    ot = ot * lax.rsqrt(jnp.mean(ot * ot, axis=0, keepdims=True) + EPS)
    o = ot.T * sg_ref[...] * (1.0 - lam_init)
    o_ref[0] = o.astype(o_ref.dtype)


def _diff_attention(proj, vt, lam_params, subln_g, lam_init, batch, seq):
    tq = DIFF_BLOCK
    heads = D_MODEL // LANES
    rows = LANES + ONES_ROWS
    return pl.pallas_call(
        functools.partial(_diff_kernel, lam_init=lam_init),
        grid=(batch, heads, seq // tq),
        in_specs=[pl.BlockSpec((1, tq, LANES), lambda b, h, i: (b, i, 2 * heads + h)),
                  pl.BlockSpec((1, seq, LANES), lambda b, h, i: (b, 0, h)),
                  pl.BlockSpec((1, LANES, seq), lambda b, h, i: (b, h, 0)),
                  pl.BlockSpec((4, HEAD_DIM), lambda b, h, i: (0, 0)),
                  pl.BlockSpec((1, LANES), lambda b, h, i: (0, 0))],
        out_specs=pl.BlockSpec((1, tq, LANES), lambda b, h, i: (b, i, h)),
        out_shape=jax.ShapeDtypeStruct((batch, seq, D_MODEL), jnp.bfloat16),
        scratch_shapes=[pltpu.VMEM((2, rows, tq), jnp.float32),
                        pltpu.VMEM((2, 1, tq), jnp.float32),
                        pltpu.VMEM((2, 1, tq), jnp.float32),
                        pltpu.VMEM((2, tq, tq), jnp.float32),
                        pltpu.VMEM((2, tq, tq), jnp.bfloat16)],
        compiler_params=_params("parallel", "parallel", "arbitrary"),
        name="diff_attention",
    )(proj, proj, vt, lam_params, subln_g.reshape(1, LANES))


def kernel(x, positions, a_norm_g, a_w_in, a_w_out, kv_norm_g, w_kv, k_norm_g, b_norm_g,
           b_w_in, b_q_norm_g, b_lambda, b_subln_g, b_w_out):
    batch, seq, d = x.shape
    assert d == D_MODEL and seq % ATT_BLOCK == 0 and seq % DIFF_BLOCK == 0
    assert seq % PROJ_ROWS == 0 and (batch * seq) % OUT_PROJ_ROWS == 0
    assert a_norm_g.shape[0] == 1 and b_norm_g.shape[0] == 1
    bf16 = jnp.bfloat16
    x2 = x.reshape(batch * seq, d)

    qkvg = _proj_a(x2, a_norm_g[0], a_w_in[0].astype(bf16))
    o = _stick_breaking(qkvg.reshape(batch, seq, 4 * d), batch, seq)
    x2 = _out_proj(qkvg, 3, o.reshape(batch * seq, d), x2, a_w_out[0].astype(bf16))

    cos, sin = _rope_tables(positions)
    w_b = jnp.concatenate([w_kv, b_w_in[0]], axis=1).astype(bf16)
    g2 = jnp.stack([kv_norm_g, b_norm_g[0]]).reshape(2, 1, d)
    ng2 = jnp.stack([jnp.tile(k_norm_g, d // HEAD_DIM),
                     jnp.tile(b_q_norm_g[0], d // HEAD_DIM)]).reshape(2, 1, d)
    proj, vt = _proj_b(x2, g2, w_b, ng2, cos, sin, batch)
    layer_idx = N_A_LAYERS
    lam_init = 0.8 - 0.6 * math.exp(-0.3 * layer_idx)
    o = _diff_attention(proj.reshape(batch, seq, 4 * d), vt, b_lambda[0], b_subln_g[0],
                        lam_init, batch, seq)
    x2 = _out_proj(proj, 3, o.reshape(batch * seq, d), x2, b_w_out[0].astype(bf16))
    return x2.reshape(batch, seq, d)
```

```python
import functools
import math

import jax
import jax.numpy as jnp
import numpy as np
from jax import lax
from jax.experimental import pallas as pl
from jax.experimental.pallas import tpu as pltpu

D_MODEL = 1024
HEAD_DIM = 64
LANES = 128
ROPE_THETA = 10000.0
EPS = 1e-6
DEPTH = 2
N_A_LAYERS = DEPTH // 2
VMEM_LIMIT_BYTES = 56 * 1024 * 1024

PROJ_ROWS = 512
OUT_PROJ_ROWS = 1024
PROJ_COLS = 1024
ATT_BLOCK = 256
SB_GROUP = 4
SB_HEADS = 8
DIFF_BLOCK = 512
ONES_ROWS = 16
NEG_BIG = -1e30
LOG2E = math.log2(math.e)
DIFF_Q_SCALE = LOG2E / math.sqrt(HEAD_DIM)
SB_Q_SCALE = -LOG2E / math.sqrt(HEAD_DIM)
UNDERFLOW_LOG2 = -150.0

_NT = (((1,), (1,)), ((), ()))


def _params(*sem):
    return pltpu.CompilerParams(dimension_semantics=sem, vmem_limit_bytes=VMEM_LIMIT_BYTES)


def _rope_kernel(pos_ref, freq_ref, cos_ref, sin_ref):
    ang = pos_ref[...].astype(jnp.float32) * freq_ref[...]
    lane = lax.broadcasted_iota(jnp.int32, ang.shape, 1)
    first_half = (lane % HEAD_DIM) < (HEAD_DIM // 2)
    cos_ref[...] = jnp.cos(ang)
    sin = jnp.sin(ang)
    sin_ref[...] = jnp.where(first_half, -sin, sin)


def _rope_tables(positions):
    n = positions.size
    rows = 1024
    inv_freq = ROPE_THETA ** (-jnp.arange(0, HEAD_DIM, 2, dtype=jnp.float32) / HEAD_DIM)
    freq = jnp.tile(inv_freq, LANES // (HEAD_DIM // 2)).reshape(1, LANES)
    out = jax.ShapeDtypeStruct((n, LANES), jnp.float32)
    return pl.pallas_call(
        _rope_kernel,
        grid=(n // rows,),
        in_specs=[pl.BlockSpec((rows, 1), lambda i: (i, 0)),
                  pl.BlockSpec((1, LANES), lambda i: (0, 0))],
        out_specs=[pl.BlockSpec((rows, LANES), lambda i: (i, 0))] * 2,
        out_shape=[out, out],
        compiler_params=_params("parallel"),
        name="rope_tables",
    )(positions.reshape(n, 1), freq)


def _rms_scale(x):
    return x * lax.rsqrt(jnp.mean(x * x, axis=-1, keepdims=True) + EPS)


def _proj_a_kernel(x_ref, g_ref, w_ref, o_ref):
    h = (_rms_scale(x_ref[...]) * g_ref[...]).astype(jnp.bfloat16)
    for c in range(o_ref.shape[1] // PROJ_COLS):
        cols = slice(c * PROJ_COLS, (c + 1) * PROJ_COLS)
        acc = jnp.dot(h, w_ref[:, cols], preferred_element_type=jnp.float32)
        if c == 0:
            acc = acc * SB_Q_SCALE
        o_ref[:, cols] = acc.astype(o_ref.dtype)


def _proj_a(x2, g, w):
    m, d = x2.shape
    n = w.shape[1]
    return pl.pallas_call(
        _proj_a_kernel,
        grid=(m // PROJ_ROWS,),
        in_specs=[pl.BlockSpec((PROJ_ROWS, d), lambda i: (i, 0)),
                  pl.BlockSpec((1, d), lambda i: (0, 0)),
                  pl.BlockSpec((d, n), lambda i: (0, 0))],
        out_specs=pl.BlockSpec((PROJ_ROWS, n), lambda i: (i, 0)),
        out_shape=jax.ShapeDtypeStruct((m, n), jnp.bfloat16),
        compiler_params=_params("parallel"),
        name="proj_a",
    )(x2, g.reshape(1, d), w)


def _split_bf16(x):
    hi = x.astype(jnp.bfloat16)
    lo = (x - hi.astype(jnp.float32)).astype(jnp.bfloat16)
    return hi, lo


def _head_norm_rope(acc, ng, cos, sin, seg_mean):
    width = seg_mean.shape[0]
    lane = lax.broadcasted_iota(jnp.int32, cos.shape, 1)
    first_half = (lane % HEAD_DIM) < (HEAD_DIM // 2)
    outs = []
    for c in range(acc.shape[1] // width):
        t = acc[:, c * width:(c + 1) * width]
        hi, lo = _split_bf16(t * t)
        ms = (jnp.dot(hi, seg_mean, preferred_element_type=jnp.float32)
              + jnp.dot(lo, seg_mean, preferred_element_type=jnp.float32))
        tn = t * lax.rsqrt(ms + EPS) * ng[:, c * width:(c + 1) * width]
        for s in range(width // LANES):
            u = tn[:, s * LANES:(s + 1) * LANES]
            swapped = jnp.where(first_half,
                                pltpu.roll(u, LANES - HEAD_DIM // 2, axis=1),
                                pltpu.roll(u, HEAD_DIM // 2, axis=1))
            outs.append(u * cos + swapped * sin)
    return jnp.concatenate(outs, axis=1)


def _gated_out_proj(gate_ref, attn_ref, x_ref, w_ref):
    gate = gate_ref[...].astype(jnp.float32)
    u = (gate * jax.nn.sigmoid(gate) * attn_ref[...].astype(jnp.float32)).astype(jnp.bfloat16)
    return x_ref[...] + jnp.dot(u, w_ref[...], preferred_element_type=jnp.float32)


def _proj_b_kernel(gate_ref, attn_ref, x_ref, wo_ref, g_ref, w_ref, ng_ref, cos_ref, sin_ref,
                   seg_ref, x1_ref, o_ref, vt_ref):
    x1 = _gated_out_proj(gate_ref, attn_ref, x_ref, wo_ref)
    x1_ref[...] = x1
    xn = _rms_scale(x1)
    for c in range(o_ref.shape[1] // PROJ_COLS):
        cols = slice(c * PROJ_COLS, (c + 1) * PROJ_COLS)
        h = (xn * g_ref[c // 2]).astype(jnp.bfloat16)
        acc = jnp.dot(h, w_ref[:, cols], preferred_element_type=jnp.float32)
        if c % 2 == 0:
            acc = _head_norm_rope(acc, ng_ref[c // 2], cos_ref[...], sin_ref[...], seg_ref[...])
        if c == 1:
            vt_ref[0] = acc.T.astype(vt_ref.dtype)
        if c == 2:
            acc = acc * DIFF_Q_SCALE
        o_ref[:, cols] = acc.astype(o_ref.dtype)


def _proj_b(prev_proj, prev_attn, x2, w_out, g2, w, ng2, cos, sin, batch):
    m, d = x2.shape
    n = w.shape[1]
    seq = m // batch
    tiles = seq // PROJ_ROWS
    width = 2 * LANES
    seg = np.arange(width)[:, None] // HEAD_DIM == np.arange(width)[None, :] // HEAD_DIM
    seg_mean = jnp.asarray(seg.astype(np.float32) / HEAD_DIM, jnp.bfloat16)
    return pl.pallas_call(
        _proj_b_kernel,
        grid=(m // PROJ_ROWS,),
        in_specs=[pl.BlockSpec((PROJ_ROWS, d), lambda i: (i, 3)),
                  pl.BlockSpec((PROJ_ROWS, d), lambda i: (i, 0)),
                  pl.BlockSpec((PROJ_ROWS, d), lambda i: (i, 0)),
                  pl.BlockSpec((d, d), lambda i: (0, 0)),
                  pl.BlockSpec((2, 1, d), lambda i: (0, 0, 0)),
                  pl.BlockSpec((d, n), lambda i: (0, 0)),
                  pl.BlockSpec((2, 1, PROJ_COLS), lambda i: (0, 0, 0)),
                  pl.BlockSpec((PROJ_ROWS, LANES), lambda i: (i, 0)),
                  pl.BlockSpec((PROJ_ROWS, LANES), lambda i: (i, 0)),
                  pl.BlockSpec((width, width), lambda i: (0, 0))],
        out_specs=[pl.BlockSpec((PROJ_ROWS, d), lambda i: (i, 0)),
                   pl.BlockSpec((PROJ_ROWS, n), lambda i: (i, 0)),
                   pl.BlockSpec((1, d, PROJ_ROWS), lambda i: (i // tiles, 0, i % tiles))],
        out_shape=[jax.ShapeDtypeStruct((m, d), jnp.float32),
                   jax.ShapeDtypeStruct((m, n), jnp.bfloat16),
                   jax.ShapeDtypeStruct((batch, d, seq), jnp.bfloat16)],
        compiler_params=_params("parallel"),
        name="proj_b",
    )(prev_proj, prev_attn, x2, w_out, g2, w, ng2, cos, sin, seg_mean)


def _out_proj_kernel(gate_ref, attn_ref, x_ref, w_ref, y_ref):
    y_ref[...] = _gated_out_proj(gate_ref, attn_ref, x_ref, w_ref)


def _out_proj(proj, gate_col_block, o, x2, w):
    m, d = x2.shape
    return pl.pallas_call(
        _out_proj_kernel,
        grid=(m // OUT_PROJ_ROWS,),
        in_specs=[pl.BlockSpec((OUT_PROJ_ROWS, d), lambda i: (i, gate_col_block)),
                  pl.BlockSpec((OUT_PROJ_ROWS, d), lambda i: (i, 0)),
                  pl.BlockSpec((OUT_PROJ_ROWS, d), lambda i: (i, 0)),
                  pl.BlockSpec((d, d), lambda i: (0, 0))],
        out_specs=pl.BlockSpec((OUT_PROJ_ROWS, d), lambda i: (i, 0)),
        out_shape=jax.ShapeDtypeStruct((m, d), jnp.float32),
        compiler_params=_params("parallel"),
        name="out_proj",
    )(proj, o, x2, w)


def _sb_kernel(q_ref, k_ref, v_ref, tri_ref, o_ref, acc_ref, c_ref, a_ref):
    tq = ATT_BLOCK
    i = pl.program_id(2)
    heads = range(SB_HEADS)
    gw = SB_GROUP * HEAD_DIM
    lane = lax.broadcasted_iota(jnp.int32, (1, gw), 1)

    def group(x, h):
        g = h // SB_GROUP
        return x[:, g * gw:(g + 1) * gw]

    qn = q_ref[0]
    qh = [jnp.where((lane // HEAD_DIM) == h % SB_GROUP, group(qn, h), 0).astype(qn.dtype)
          for h in heads]
    tri2 = tri_ref[...]

    def tile(ref, j):
        return ref[0, pl.ds(pl.multiple_of(j * tq, tq), tq), :]

    def scores(j):
        kb = tile(k_ref, j)
        return [lax.dot_general(qh[h], group(kb, h), _NT, preferred_element_type=jnp.float32)
                for h in heads]

    def weights(n, carry, strict):
        l = jnp.minimum(n, 0.0) - jnp.log2(1.0 + jnp.exp2(-jnp.abs(n)))
        if strict is not None:
            l = jnp.where(strict, l, 0.0)
        hi, lo = _split_bf16(l)
        incl = jnp.dot(jnp.concatenate([hi, lo], axis=1), tri2,
                       preferred_element_type=jnp.float32)
        a = jnp.exp2(incl - n + carry)
        if strict is not None:
            a = jnp.where(strict, a, 0.0)
        return a.astype(jnp.bfloat16), carry + incl[:, :1]

    def step(j, strict, first):
        n = scores(j)
        if not first:
            vb = tile(v_ref, j + 1)
            for h in heads:
                acc_ref[h] += jnp.dot(a_ref[h], group(vb, h),
                                      preferred_element_type=jnp.float32)
        carries = []
        for h in heads:
            carry = jnp.zeros((tq, 1), jnp.float32) if first else c_ref[h]
            a_ref[h], carry = weights(n[h], carry, strict)
            c_ref[h] = carry
            carries.append(carry)
        return jnp.max(functools.reduce(jnp.maximum, carries))

    acc_ref[...] = jnp.zeros_like(acc_ref)
    row = lax.broadcasted_iota(jnp.int32, (tq, tq), 0)
    col = lax.broadcasted_iota(jnp.int32, (tq, tq), 1)
    stick = step(i, col < row, True)

    def cond(state):
        t, stick = state
        return jnp.logical_and(t < i, stick > UNDERFLOW_LOG2)

    def body(state):
        t, _ = state
        return t + 1, step(i - 1 - t, None, False)

    done, _ = lax.while_loop(cond, body, (jnp.int32(0), stick))
    vb = tile(v_ref, i - done)
    out = [acc_ref[h] + jnp.dot(a_ref[h], group(vb, h), preferred_element_type=jnp.float32)
           for h in heads]
    res = []
    for g in range(SB_HEADS // SB_GROUP):
        r = out[g * SB_GROUP]
        for h in range(1, SB_GROUP):
            r = jnp.where((lane // HEAD_DIM) == h, out[g * SB_GROUP + h], r)
        res.append(r)
    o_ref[0] = jnp.concatenate(res, axis=1).astype(o_ref.dtype)


def _stick_breaking(qkvg, batch, seq):
    tq = ATT_BLOCK
    width = SB_HEADS * HEAD_DIM
    blocks = D_MODEL // width
    tri = (np.arange(tq)[:, None] >= np.arange(tq)[None, :]).astype(np.float32)
    tri2 = jnp.asarray(np.concatenate([tri, tri], axis=0), jnp.bfloat16)
    return pl.pallas_call(
        _sb_kernel,
        grid=(batch, blocks, seq // tq),
        in_specs=[pl.BlockSpec((1, tq, width), lambda b, h, i: (b, i, h)),
                  pl.BlockSpec((1, seq, width), lambda b, h, i: (b, 0, blocks + h)),
                  pl.BlockSpec((1, seq, width), lambda b, h, i: (b, 0, 2 * blocks + h)),
                  pl.BlockSpec((2 * tq, tq), lambda b, h, i: (0, 0))],
        out_specs=pl.BlockSpec((1, tq, width), lambda b, h, i: (b, i, h)),
        out_shape=jax.ShapeDtypeStruct((batch, seq, D_MODEL), jnp.bfloat16),
        scratch_shapes=[pltpu.VMEM((SB_HEADS, tq, SB_GROUP * HEAD_DIM), jnp.float32),
                        pltpu.VMEM((SB_HEADS, tq, 1), jnp.float32),
                        pltpu.VMEM((SB_HEADS, tq, tq), jnp.bfloat16)],
        compiler_params=_params("parallel", "parallel", "arbitrary"),
        name="stick_breaking",
    )(qkvg, qkvg, qkvg, tri2)


def _diff_kernel(q_ref, k_ref, vt_ref, lam_ref, sg_ref, o_ref, acc_ref, m_ref, alpha_ref, s_ref,
                 p_ref, *, lam_init):
    tq = tk = DIFF_BLOCK
    i = pl.program_id(2)
    comps = range(2)
    lane = lax.broadcasted_iota(jnp.int32, (1, LANES), 1)
    qs = q_ref[0]
    qc = [jnp.where((lane // HEAD_DIM) == c, qs, jnp.zeros_like(qs)) for c in comps]
    ones = jnp.ones((ONES_ROWS, tk), jnp.bfloat16)

    def scores(j):
        kb = k_ref[0, pl.ds(pl.multiple_of(j * tk, tk), tk), :]
        return [lax.dot_general(kb, qc[c], _NT, preferred_element_type=jnp.float32)
                for c in comps]

    def accumulate(acc, alpha, p, j):
        vt = vt_ref[0, :, pl.ds(pl.multiple_of(j * tk, tk), tk)]
        vt1 = jnp.concatenate([vt, ones], axis=0)
        return [alpha[c] * acc[c] + jnp.dot(vt1, p[c], preferred_element_type=jnp.float32)
                for c in comps]

    def softmax_tile(s, m, causal):
        p, alpha, m_out = [], [], []
        for c in comps:
            sc = s[c] if causal is None else jnp.where(causal, s[c], NEG_BIG)
            m_new = jnp.maximum(m[c], jnp.max(sc, axis=0, keepdims=True))
            alpha.append(jnp.exp2(m[c] - m_new))
            p.append(jnp.exp2(sc - m_new).astype(jnp.bfloat16))
            m_out.append(m_new)
        return p, alpha, m_out

    def load(ref):
        return [ref[c] for c in comps]

    def store(ref, vals):
        for c in comps:
            ref[c] = vals[c]

    def step(j):
        s_next = scores(jnp.maximum(j - 1, 0))
        acc = accumulate(load(acc_ref), load(alpha_ref), load(p_ref), j + 1)
        p, alpha, m = softmax_tile(load(s_ref), load(m_ref), None)
        store(acc_ref, acc), store(p_ref, p), store(alpha_ref, alpha), store(m_ref, m)
        store(s_ref, s_next)

    key = lax.broadcasted_iota(jnp.int32, (tk, tq), 0)
    qry = lax.broadcasted_iota(jnp.int32, (tk, tq), 1)
    s_diag = scores(i)
    s_next = scores(jnp.maximum(i - 1, 0))
    m0 = [jnp.full((1, tq), NEG_BIG, jnp.float32) for c in comps]
    p, alpha, m = softmax_tile(s_diag, m0, key <= qry)
    store(p_ref, p), store(alpha_ref, alpha), store(m_ref, m), store(s_ref, s_next)
    acc_ref[...] = jnp.zeros_like(acc_ref)

    def body(t, _):
        step(i - 1 - t)
        return 0

    lax.fori_loop(0, i, body, 0)
    acc = accumulate(load(acc_ref), load(alpha_ref), load(p_ref), 0)

    lp = lam_ref[...]
    lam = (jnp.exp(jnp.sum(lp[0:1] * lp[1:2], axis=-1, keepdims=True))
           - jnp.exp(jnp.sum(lp[2:3] * lp[3:4], axis=-1, keepdims=True)) + lam_init)
    num = [acc[c][:LANES] for c in comps]
    den = [acc[c][LANES:LANES + 1] for c in comps]
    ot = num[0] / den[0] - lam * (num[1] / den[1])
    ot = ot * lax.rsqrt(jnp.mean(ot * ot, axis=0, keepdims=True) + EPS)
    o = ot.T * sg_ref[...] * (1.0 - lam_init)
    o_ref[0] = o.astype(o_ref.dtype)


def _diff_attention(proj, vt, lam_params, subln_g, lam_init, batch, seq):
    tq = DIFF_BLOCK
    heads = D_MODEL // LANES
    rows = LANES + ONES_ROWS
    return pl.pallas_call(
        functools.partial(_diff_kernel, lam_init=lam_init),
        grid=(batch, heads, seq // tq),
        in_specs=[pl.BlockSpec((1, tq, LANES), lambda b, h, i: (b, i, 2 * heads + h)),
                  pl.BlockSpec((1, seq, LANES), lambda b, h, i: (b, 0, h)),
                  pl.BlockSpec((1, LANES, seq), lambda b, h, i: (b, h, 0)),
                  pl.BlockSpec((4, HEAD_DIM), lambda b, h, i: (0, 0)),
                  pl.BlockSpec((1, LANES), lambda b, h, i: (0, 0))],
        out_specs=pl.BlockSpec((1, tq, LANES), lambda b, h, i: (b, i, h)),
        out_shape=jax.ShapeDtypeStruct((batch, seq, D_MODEL), jnp.bfloat16),
        scratch_shapes=[pltpu.VMEM((2, rows, tq), jnp.float32),
                        pltpu.VMEM((2, 1, tq), jnp.float32),
                        pltpu.VMEM((2, 1, tq), jnp.float32),
                        pltpu.VMEM((2, tq, tq), jnp.float32),
                        pltpu.VMEM((2, tq, tq), jnp.bfloat16)],
        compiler_params=_params("parallel", "parallel", "arbitrary"),
        name="diff_attention",
    )(proj, proj, vt, lam_params, subln_g.reshape(1, LANES))


def kernel(x, positions, a_norm_g, a_w_in, a_w_out, kv_norm_g, w_kv, k_norm_g, b_norm_g,
           b_w_in, b_q_norm_g, b_lambda, b_subln_g, b_w_out):
    batch, seq, d = x.shape
    assert d == D_MODEL and seq % ATT_BLOCK == 0 and seq % DIFF_BLOCK == 0
    assert seq % PROJ_ROWS == 0 and (batch * seq) % OUT_PROJ_ROWS == 0
    assert a_norm_g.shape[0] == 1 and b_norm_g.shape[0] == 1
    bf16 = jnp.bfloat16
    x2 = x.reshape(batch * seq, d)

    qkvg = _proj_a(x2, a_norm_g[0], a_w_in[0].astype(bf16))
    o = _stick_breaking(qkvg.reshape(batch, seq, 4 * d), batch, seq)

    cos, sin = _rope_tables(positions)
    w_b = jnp.concatenate([w_kv, b_w_in[0]], axis=1).astype(bf16)
    g2 = jnp.stack([kv_norm_g, b_norm_g[0]]).reshape(2, 1, d)
    ng2 = jnp.stack([jnp.tile(k_norm_g, d // HEAD_DIM),
                     jnp.tile(b_q_norm_g[0], d // HEAD_DIM)]).reshape(2, 1, d)
    x2, proj, vt = _proj_b(qkvg, o.reshape(batch * seq, d), x2, a_w_out[0].astype(bf16), g2, w_b,
                           ng2, cos, sin, batch)
    layer_idx = N_A_LAYERS
    lam_init = 0.8 - 0.6 * math.exp(-0.3 * layer_idx)
    o = _diff_attention(proj.reshape(batch, seq, 4 * d), vt, b_lambda[0], b_subln_g[0],
                        lam_init, batch, seq)
    x2 = _out_proj(proj, 3, o.reshape(batch * seq, d), x2, b_w_out[0].astype(bf16))
    return x2.reshape(batch, seq, d)
```

```python
import functools
import math

import jax
import jax.numpy as jnp
import numpy as np
from jax import lax
from jax.experimental import pallas as pl
from jax.experimental.pallas import tpu as pltpu

D_MODEL = 1024
HEAD_DIM = 64
LANES = 128
ROPE_THETA = 10000.0
EPS = 1e-6
DEPTH = 2
N_A_LAYERS = DEPTH // 2
VMEM_LIMIT_BYTES = 56 * 1024 * 1024

PROJ_ROWS = 512
OUT_PROJ_ROWS = 1024
PROJ_COLS = 1024
ATT_BLOCK = 256
SB_GROUP = 4
SB_HEADS = 8
DIFF_BLOCK = 512
DIFF_HEADS = 2
ONES_ROWS = 16
NEG_BIG = -1e30
LOG2E = math.log2(math.e)
DIFF_Q_SCALE = LOG2E / math.sqrt(HEAD_DIM)
SB_Q_SCALE = -LOG2E / math.sqrt(HEAD_DIM)
UNDERFLOW_LOG2 = -150.0

_NT = (((1,), (1,)), ((), ()))


def _params(*sem):
    return pltpu.CompilerParams(dimension_semantics=sem, vmem_limit_bytes=VMEM_LIMIT_BYTES)


def _rope_kernel(pos_ref, freq_ref, cos_ref, sin_ref):
    ang = pos_ref[...].astype(jnp.float32) * freq_ref[...]
    lane = lax.broadcasted_iota(jnp.int32, ang.shape, 1)
    first_half = (lane % HEAD_DIM) < (HEAD_DIM // 2)
    cos_ref[...] = jnp.cos(ang)
    sin = jnp.sin(ang)
    sin_ref[...] = jnp.where(first_half, -sin, sin)


def _rope_tables(positions):
    n = positions.size
    rows = 1024
    inv_freq = ROPE_THETA ** (-jnp.arange(0, HEAD_DIM, 2, dtype=jnp.float32) / HEAD_DIM)
    freq = jnp.tile(inv_freq, LANES // (HEAD_DIM // 2)).reshape(1, LANES)
    out = jax.ShapeDtypeStruct((n, LANES), jnp.float32)
    return pl.pallas_call(
        _rope_kernel,
        grid=(n // rows,),
        in_specs=[pl.BlockSpec((rows, 1), lambda i: (i, 0)),
                  pl.BlockSpec((1, LANES), lambda i: (0, 0))],
        out_specs=[pl.BlockSpec((rows, LANES), lambda i: (i, 0))] * 2,
        out_shape=[out, out],
        compiler_params=_params("parallel"),
        name="rope_tables",
    )(positions.reshape(n, 1), freq)


def _rms_scale(x):
    return x * lax.rsqrt(jnp.mean(x * x, axis=-1, keepdims=True) + EPS)


def _proj_a_kernel(x_ref, g_ref, w_ref, o_ref):
    h = (_rms_scale(x_ref[...]) * g_ref[...]).astype(jnp.bfloat16)
    for c in range(o_ref.shape[1] // PROJ_COLS):
        cols = slice(c * PROJ_COLS, (c + 1) * PROJ_COLS)
        acc = jnp.dot(h, w_ref[:, cols], preferred_element_type=jnp.float32)
        if c == 0:
            acc = acc * SB_Q_SCALE
        o_ref[:, cols] = acc.astype(o_ref.dtype)


def _proj_a(x2, g, w):
    m, d = x2.shape
    n = w.shape[1]
    return pl.pallas_call(
        _proj_a_kernel,
        grid=(m // PROJ_ROWS,),
        in_specs=[pl.BlockSpec((PROJ_ROWS, d), lambda i: (i, 0)),
                  pl.BlockSpec((1, d), lambda i: (0, 0)),
                  pl.BlockSpec((d, n), lambda i: (0, 0))],
        out_specs=pl.BlockSpec((PROJ_ROWS, n), lambda i: (i, 0)),
        out_shape=jax.ShapeDtypeStruct((m, n), jnp.bfloat16),
        compiler_params=_params("parallel"),
        name="proj_a",
    )(x2, g.reshape(1, d), w)


def _split_bf16(x):
    hi = x.astype(jnp.bfloat16)
    lo = (x - hi.astype(jnp.float32)).astype(jnp.bfloat16)
    return hi, lo


def _head_norm_rope(acc, ng, cos, sin, seg_mean):
    width = seg_mean.shape[0]
    lane = lax.broadcasted_iota(jnp.int32, cos.shape, 1)
    first_half = (lane % HEAD_DIM) < (HEAD_DIM // 2)
    outs = []
    for c in range(acc.shape[1] // width):
        t = acc[:, c * width:(c + 1) * width]
        hi, lo = _split_bf16(t * t)
        ms = (jnp.dot(hi, seg_mean, preferred_element_type=jnp.float32)
              + jnp.dot(lo, seg_mean, preferred_element_type=jnp.float32))
        tn = t * lax.rsqrt(ms + EPS) * ng[:, c * width:(c + 1) * width]
        for s in range(width // LANES):
            u = tn[:, s * LANES:(s + 1) * LANES]
            swapped = jnp.where(first_half,
                                pltpu.roll(u, LANES - HEAD_DIM // 2, axis=1),
                                pltpu.roll(u, HEAD_DIM // 2, axis=1))
            outs.append(u * cos + swapped * sin)
    return jnp.concatenate(outs, axis=1)


def _gated_out_proj(gate_ref, attn_ref, x_ref, w_ref):
    gate = gate_ref[...].astype(jnp.float32)
    u = (gate * jax.nn.sigmoid(gate) * attn_ref[...].astype(jnp.float32)).astype(jnp.bfloat16)
    return x_ref[...] + jnp.dot(u, w_ref[...], preferred_element_type=jnp.float32)


def _proj_b_kernel(gate_ref, attn_ref, x_ref, wo_ref, g_ref, w_ref, ng_ref, cos_ref, sin_ref,
                   seg_ref, x1_ref, o_ref, vt_ref):
    x1 = _gated_out_proj(gate_ref, attn_ref, x_ref, wo_ref)
    x1_ref[...] = x1
    xn = _rms_scale(x1)
    for c in range(o_ref.shape[1] // PROJ_COLS):
        cols = slice(c * PROJ_COLS, (c + 1) * PROJ_COLS)
        h = (xn * g_ref[c // 2]).astype(jnp.bfloat16)
        acc = jnp.dot(h, w_ref[:, cols], preferred_element_type=jnp.float32)
        if c % 2 == 0:
            acc = _head_norm_rope(acc, ng_ref[c // 2], cos_ref[...], sin_ref[...], seg_ref[...])
        if c == 1:
            vt_ref[0] = acc.T.astype(vt_ref.dtype)
        if c == 2:
            acc = acc * DIFF_Q_SCALE
        o_ref[:, cols] = acc.astype(o_ref.dtype)


def _proj_b(prev_proj, prev_attn, x2, w_out, g2, w, ng2, cos, sin, batch):
    m, d = x2.shape
    n = w.shape[1]
    seq = m // batch
    tiles = seq // PROJ_ROWS
    width = 2 * LANES
    seg = np.arange(width)[:, None] // HEAD_DIM == np.arange(width)[None, :] // HEAD_DIM
    seg_mean = jnp.asarray(seg.astype(np.float32) / HEAD_DIM, jnp.bfloat16)
    return pl.pallas_call(
        _proj_b_kernel,
        grid=(m // PROJ_ROWS,),
        in_specs=[pl.BlockSpec((PROJ_ROWS, d), lambda i: (i, 3)),
                  pl.BlockSpec((PROJ_ROWS, d), lambda i: (i, 0)),
                  pl.BlockSpec((PROJ_ROWS, d), lambda i: (i, 0)),
                  pl.BlockSpec((d, d), lambda i: (0, 0)),
                  pl.BlockSpec((2, 1, d), lambda i: (0, 0, 0)),
                  pl.BlockSpec((d, n), lambda i: (0, 0)),
                  pl.BlockSpec((2, 1, PROJ_COLS), lambda i: (0, 0, 0)),
                  pl.BlockSpec((PROJ_ROWS, LANES), lambda i: (i, 0)),
                  pl.BlockSpec((PROJ_ROWS, LANES), lambda i: (i, 0)),
                  pl.BlockSpec((width, width), lambda i: (0, 0))],
        out_specs=[pl.BlockSpec((PROJ_ROWS, d), lambda i: (i, 0)),
                   pl.BlockSpec((PROJ_ROWS, n), lambda i: (i, 0)),
                   pl.BlockSpec((1, d, PROJ_ROWS), lambda i: (i // tiles, 0, i % tiles))],
        out_shape=[jax.ShapeDtypeStruct((m, d), jnp.float32),
                   jax.ShapeDtypeStruct((m, n), jnp.bfloat16),
                   jax.ShapeDtypeStruct((batch, d, seq), jnp.bfloat16)],
        compiler_params=_params("parallel"),
        name="proj_b",
    )(prev_proj, prev_attn, x2, w_out, g2, w, ng2, cos, sin, seg_mean)


def _out_proj_kernel(gate_ref, attn_ref, x_ref, w_ref, y_ref):
    y_ref[...] = _gated_out_proj(gate_ref, attn_ref, x_ref, w_ref)


def _out_proj(proj, gate_col_block, o, x2, w):
    m, d = x2.shape
    return pl.pallas_call(
        _out_proj_kernel,
        grid=(m // OUT_PROJ_ROWS,),
        in_specs=[pl.BlockSpec((OUT_PROJ_ROWS, d), lambda i: (i, gate_col_block)),
                  pl.BlockSpec((OUT_PROJ_ROWS, d), lambda i: (i, 0)),
                  pl.BlockSpec((OUT_PROJ_ROWS, d), lambda i: (i, 0)),
                  pl.BlockSpec((d, d), lambda i: (0, 0))],
        out_specs=pl.BlockSpec((OUT_PROJ_ROWS, d), lambda i: (i, 0)),
        out_shape=jax.ShapeDtypeStruct((m, d), jnp.float32),
        compiler_params=_params("parallel"),
        name="out_proj",
    )(proj, o, x2, w)


def _sb_kernel(q_ref, k_ref, v_ref, tri_ref, o_ref, acc_ref, c_ref, a_ref):
    tq = ATT_BLOCK
    i = pl.program_id(2)
    heads = range(SB_HEADS)
    gw = SB_GROUP * HEAD_DIM
    lane = lax.broadcasted_iota(jnp.int32, (1, gw), 1)

    def group(x, h):
        g = h // SB_GROUP
        return x[:, g * gw:(g + 1) * gw]

    qn = q_ref[0]
    qh = [jnp.where((lane // HEAD_DIM) == h % SB_GROUP, group(qn, h), 0).astype(qn.dtype)
          for h in heads]
    tri2 = tri_ref[...]

    def tile(ref, j):
        return ref[0, pl.ds(pl.multiple_of(j * tq, tq), tq), :]

    def scores(j):
        kb = tile(k_ref, j)
        return [lax.dot_general(qh[h], group(kb, h), _NT, preferred_element_type=jnp.float32)
                for h in heads]

    def weights(n, carry, strict):
        l = jnp.minimum(n, 0.0) - jnp.log2(1.0 + jnp.exp2(-jnp.abs(n)))
        if strict is not None:
            l = jnp.where(strict, l, 0.0)
        hi, lo = _split_bf16(l)
        incl = jnp.dot(jnp.concatenate([hi, lo], axis=1), tri2,
                       preferred_element_type=jnp.float32)
        a = jnp.exp2(incl - n + carry)
        if strict is not None:
            a = jnp.where(strict, a, 0.0)
        return a.astype(jnp.bfloat16), carry + incl[:, :1]

    def step(j, strict, first):
        n = scores(j)
        if not first:
            vb = tile(v_ref, j + 1)
            for h in heads:
                acc_ref[h] += jnp.dot(a_ref[h], group(vb, h),
                                      preferred_element_type=jnp.float32)
        carries = []
        for h in heads:
            carry = jnp.zeros((tq, 1), jnp.float32) if first else c_ref[h]
            a_ref[h], carry = weights(n[h], carry, strict)
            c_ref[h] = carry
            carries.append(carry)
        return jnp.max(functools.reduce(jnp.maximum, carries))

    acc_ref[...] = jnp.zeros_like(acc_ref)
    row = lax.broadcasted_iota(jnp.int32, (tq, tq), 0)
    col = lax.broadcasted_iota(jnp.int32, (tq, tq), 1)
    stick = step(i, col < row, True)

    def cond(state):
        t, stick = state
        return jnp.logical_and(t < i, stick > UNDERFLOW_LOG2)

    def body(state):
        t, _ = state
        return t + 1, step(i - 1 - t, None, False)

    done, _ = lax.while_loop(cond, body, (jnp.int32(0), stick))
    vb = tile(v_ref, i - done)
    out = [acc_ref[h] + jnp.dot(a_ref[h], group(vb, h), preferred_element_type=jnp.float32)
           for h in heads]
    res = []
    for g in range(SB_HEADS // SB_GROUP):
        r = out[g * SB_GROUP]
        for h in range(1, SB_GROUP):
            r = jnp.where((lane // HEAD_DIM) == h, out[g * SB_GROUP + h], r)
        res.append(r)
    o_ref[0] = jnp.concatenate(res, axis=1).astype(o_ref.dtype)


def _stick_breaking(qkvg, batch, seq):
    tq = ATT_BLOCK
    width = SB_HEADS * HEAD_DIM
    blocks = D_MODEL // width
    tri = (np.arange(tq)[:, None] >= np.arange(tq)[None, :]).astype(np.float32)
    tri2 = jnp.asarray(np.concatenate([tri, tri], axis=0), jnp.bfloat16)
    return pl.pallas_call(
        _sb_kernel,
        grid=(batch, blocks, seq // tq),
        in_specs=[pl.BlockSpec((1, tq, width), lambda b, h, i: (b, i, h)),
                  pl.BlockSpec((1, seq, width), lambda b, h, i: (b, 0, blocks + h)),
                  pl.BlockSpec((1, seq, width), lambda b, h, i: (b, 0, 2 * blocks + h)),
                  pl.BlockSpec((2 * tq, tq), lambda b, h, i: (0, 0))],
        out_specs=pl.BlockSpec((1, tq, width), lambda b, h, i: (b, i, h)),
        out_shape=jax.ShapeDtypeStruct((batch, seq, D_MODEL), jnp.bfloat16),
        scratch_shapes=[pltpu.VMEM((SB_HEADS, tq, SB_GROUP * HEAD_DIM), jnp.float32),
                        pltpu.VMEM((SB_HEADS, tq, 1), jnp.float32),
                        pltpu.VMEM((SB_HEADS, tq, tq), jnp.bfloat16)],
        compiler_params=_params("parallel", "parallel", "arbitrary"),
        name="stick_breaking",
    )(qkvg, qkvg, qkvg, tri2)


def _diff_kernel(q_ref, k_ref, vt_ref, lam_ref, sg_ref, o_ref, acc_ref, m_ref, alpha_ref, s_ref,
                 p_ref, *, lam_init):
    tq = tk = DIFF_BLOCK
    i = pl.program_id(2)
    pairs = tuple(range(2 * DIFF_HEADS))
    lane = lax.broadcasted_iota(jnp.int32, (1, LANES), 1)

    def head_lanes(c):
        return slice(c // 2 * LANES, (c // 2 + 1) * LANES)

    qs = q_ref[0]
    qc = [jnp.where((lane // HEAD_DIM) == c % 2, qs[:, head_lanes(c)], 0).astype(qs.dtype)
          for c in pairs]
    ones = jnp.ones((ONES_ROWS, tk), jnp.bfloat16)

    def scores(j, cs):
        keys = pl.ds(pl.multiple_of(j * tk, tk), tk)
        return [lax.dot_general(k_ref[0, keys, head_lanes(c)], qc[c], _NT,
                                preferred_element_type=jnp.float32) for c in cs]

    def accumulate(acc, alpha, p, j, cs):
        keys = pl.ds(pl.multiple_of(j * tk, tk), tk)
        out = []
        for n, c in enumerate(cs):
            vt1 = jnp.concatenate([vt_ref[0, head_lanes(c), keys], ones], axis=0)
            out.append(alpha[n] * acc[n] + jnp.dot(vt1, p[n], preferred_element_type=jnp.float32))
        return out

    def softmax_tile(s, m, causal):
        p, alpha, m_out = [], [], []
        for sc, m_old in zip(s, m):
            if causal is not None:
                sc = jnp.where(causal, sc, NEG_BIG)
            m_new = jnp.maximum(m_old, jnp.max(sc, axis=0, keepdims=True))
            alpha.append(jnp.exp2(m_old - m_new))
            p.append(jnp.exp2(sc - m_new).astype(jnp.bfloat16))
            m_out.append(m_new)
        return p, alpha, m_out

    def load(ref, cs):
        return [ref[c] for c in cs]

    def store(ref, vals, cs):
        for c, val in zip(cs, vals):
            ref[c] = val

    def step(j, cs):
        s_next = scores(jnp.maximum(j - 1, 0), cs)
        acc = accumulate(load(acc_ref, cs), load(alpha_ref, cs), load(p_ref, cs), j + 1, cs)
        p, alpha, m = softmax_tile(load(s_ref, cs), load(m_ref, cs), None)
        store(acc_ref, acc, cs), store(p_ref, p, cs), store(alpha_ref, alpha, cs)
        store(m_ref, m, cs), store(s_ref, s_next, cs)

    key = lax.broadcasted_iota(jnp.int32, (tk, tq), 0)
    qry = lax.broadcasted_iota(jnp.int32, (tk, tq), 1)
    s_diag = scores(i, pairs)
    s_next = scores(jnp.maximum(i - 1, 0), pairs)
    m0 = [jnp.full((1, tq), NEG_BIG, jnp.float32) for c in pairs]
    p, alpha, m = softmax_tile(s_diag, m0, key <= qry)
    store(p_ref, p, pairs), store(alpha_ref, alpha, pairs), store(m_ref, m, pairs)
    store(s_ref, s_next, pairs)
    acc_ref[...] = jnp.zeros_like(acc_ref)

    for h in range(DIFF_HEADS):
        def body(t, _, cs=(2 * h, 2 * h + 1)):
            step(i - 1 - t, cs)
            return 0

        lax.fori_loop(0, i, body, 0)
    acc = accumulate(load(acc_ref, pairs), load(alpha_ref, pairs), load(p_ref, pairs), 0, pairs)

    lp = lam_ref[...]
    lam = (jnp.exp(jnp.sum(lp[0:1] * lp[1:2], axis=-1, keepdims=True))
           - jnp.exp(jnp.sum(lp[2:3] * lp[3:4], axis=-1, keepdims=True)) + lam_init)
    num = [acc[c][:LANES] for c in pairs]
    den = [acc[c][LANES:LANES + 1] for c in pairs]
    outs = []
    for h in range(DIFF_HEADS):
        c1, c2 = 2 * h, 2 * h + 1
        ot = num[c1] / den[c1] - lam * (num[c2] / den[c2])
        ot = ot * lax.rsqrt(jnp.mean(ot * ot, axis=0, keepdims=True) + EPS)
        outs.append(ot.T * sg_ref[...] * (1.0 - lam_init))
    o_ref[0] = jnp.concatenate(outs, axis=1).astype(o_ref.dtype)


def _diff_attention(proj, vt, lam_params, subln_g, lam_init, batch, seq):
    tq = DIFF_BLOCK
    width = DIFF_HEADS * LANES
    blocks = D_MODEL // width
    rows = LANES + ONES_ROWS
    pairs = 2 * DIFF_HEADS
    return pl.pallas_call(
        functools.partial(_diff_kernel, lam_init=lam_init),
        grid=(batch, blocks, seq // tq),
        in_specs=[pl.BlockSpec((1, tq, width), lambda b, h, i: (b, i, 2 * blocks + h)),
                  pl.BlockSpec((1, seq, width), lambda b, h, i: (b, 0, h)),
                  pl.BlockSpec((1, width, seq), lambda b, h, i: (b, h, 0)),
                  pl.BlockSpec((4, HEAD_DIM), lambda b, h, i: (0, 0)),
                  pl.BlockSpec((1, LANES), lambda b, h, i: (0, 0))],
        out_specs=pl.BlockSpec((1, tq, width), lambda b, h, i: (b, i, h)),
        out_shape=jax.ShapeDtypeStruct((batch, seq, D_MODEL), jnp.bfloat16),
        scratch_shapes=[pltpu.VMEM((pairs, rows, tq), jnp.float32),
                        pltpu.VMEM((pairs, 1, tq), jnp.float32),
                        pltpu.VMEM((pairs, 1, tq), jnp.float32),
                        pltpu.VMEM((pairs, tq, tq), jnp.float32),
                        pltpu.VMEM((pairs, tq, tq), jnp.bfloat16)],
        compiler_params=_params("parallel", "parallel", "arbitrary"),
        name="diff_attention",
    )(proj, proj, vt, lam_params, subln_g.reshape(1, LANES))


def kernel(x, positions, a_norm_g, a_w_in, a_w_out, kv_norm_g, w_kv, k_norm_g, b_norm_g,
           b_w_in, b_q_norm_g, b_lambda, b_subln_g, b_w_out):
    batch, seq, d = x.shape
    assert d == D_MODEL and seq % ATT_BLOCK == 0 and seq % DIFF_BLOCK == 0
    assert seq % PROJ_ROWS == 0 and (batch * seq) % OUT_PROJ_ROWS == 0
    assert a_norm_g.shape[0] == 1 and b_norm_g.shape[0] == 1
    bf16 = jnp.bfloat16
    x2 = x.reshape(batch * seq, d)

    qkvg = _proj_a(x2, a_norm_g[0], a_w_in[0].astype(bf16))
    o = _stick_breaking(qkvg.reshape(batch, seq, 4 * d), batch, seq)

    cos, sin = _rope_tables(positions)
    w_b = jnp.concatenate([w_kv, b_w_in[0]], axis=1).astype(bf16)
    g2 = jnp.stack([kv_norm_g, b_norm_g[0]]).reshape(2, 1, d)
    ng2 = jnp.stack([jnp.tile(k_norm_g, d // HEAD_DIM),
                     jnp.tile(b_q_norm_g[0], d // HEAD_DIM)]).reshape(2, 1, d)
    x2, proj, vt = _proj_b(qkvg, o.reshape(batch * seq, d), x2, a_w_out[0].astype(bf16), g2, w_b,
                           ng2, cos, sin, batch)
    layer_idx = N_A_LAYERS
    lam_init = 0.8 - 0.6 * math.exp(-0.3 * layer_idx)
    o = _diff_attention(proj.reshape(batch, seq, 4 * d), vt, b_lambda[0], b_subln_g[0],
                        lam_init, batch, seq)
    x2 = _out_proj(proj, 3, o.reshape(batch * seq, d), x2, b_w_out[0].astype(bf16))
    return x2.reshape(batch, seq, d)
```

```python
import functools
import math

import jax
import jax.numpy as jnp
import numpy as np
from jax import lax
from jax.experimental import pallas as pl
from jax.experimental.pallas import tpu as pltpu

D_MODEL = 1024
HEAD_DIM = 64
LANES = 128
ROPE_THETA = 10000.0
EPS = 1e-6
DEPTH = 2
N_A_LAYERS = DEPTH // 2
VMEM_LIMIT_BYTES = 56 * 1024 * 1024

PROJ_ROWS = 512
OUT_PROJ_ROWS = 1024
PROJ_COLS = 1024
ATT_BLOCK = 256
SB_GROUP = 4
SB_HEADS = 8
DIFF_BLOCK = 512
DIFF_HEADS = 4
ONES_ROWS = 16
NEG_BIG = -1e30
LOG2E = math.log2(math.e)
DIFF_Q_SCALE = LOG2E / math.sqrt(HEAD_DIM)
SB_Q_SCALE = -LOG2E / math.sqrt(HEAD_DIM)
UNDERFLOW_LOG2 = -150.0

_NT = (((1,), (1,)), ((), ()))


def _params(*sem):
    return pltpu.CompilerParams(dimension_semantics=sem, vmem_limit_bytes=VMEM_LIMIT_BYTES)


def _rope_kernel(pos_ref, freq_ref, cos_ref, sin_ref):
    ang = pos_ref[...].astype(jnp.float32) * freq_ref[...]
    lane = lax.broadcasted_iota(jnp.int32, ang.shape, 1)
    first_half = (lane % HEAD_DIM) < (HEAD_DIM // 2)
    cos_ref[...] = jnp.cos(ang)
    sin = jnp.sin(ang)
    sin_ref[...] = jnp.where(first_half, -sin, sin)


def _rope_tables(positions):
    n = positions.size
    rows = 1024
    inv_freq = ROPE_THETA ** (-jnp.arange(0, HEAD_DIM, 2, dtype=jnp.float32) / HEAD_DIM)
    freq = jnp.tile(inv_freq, LANES // (HEAD_DIM // 2)).reshape(1, LANES)
    out = jax.ShapeDtypeStruct((n, LANES), jnp.float32)
    return pl.pallas_call(
        _rope_kernel,
        grid=(n // rows,),
        in_specs=[pl.BlockSpec((rows, 1), lambda i: (i, 0)),
                  pl.BlockSpec((1, LANES), lambda i: (0, 0))],
        out_specs=[pl.BlockSpec((rows, LANES), lambda i: (i, 0))] * 2,
        out_shape=[out, out],
        compiler_params=_params("parallel"),
        name="rope_tables",
    )(positions.reshape(n, 1), freq)


def _rms_scale(x):
    return x * lax.rsqrt(jnp.mean(x * x, axis=-1, keepdims=True) + EPS)


def _proj_a_kernel(x_ref, g_ref, w_ref, o_ref):
    h = (_rms_scale(x_ref[...]) * g_ref[...]).astype(jnp.bfloat16)
    for c in range(o_ref.shape[1] // PROJ_COLS):
        cols = slice(c * PROJ_COLS, (c + 1) * PROJ_COLS)
        acc = jnp.dot(h, w_ref[:, cols], preferred_element_type=jnp.float32)
        if c == 0:
            acc = acc * SB_Q_SCALE
        o_ref[:, cols] = acc.astype(o_ref.dtype)


def _proj_a(x2, g, w):
    m, d = x2.shape
    n = w.shape[1]
    return pl.pallas_call(
        _proj_a_kernel,
        grid=(m // PROJ_ROWS,),
        in_specs=[pl.BlockSpec((PROJ_ROWS, d), lambda i: (i, 0)),
                  pl.BlockSpec((1, d), lambda i: (0, 0)),
                  pl.BlockSpec((d, n), lambda i: (0, 0))],
        out_specs=pl.BlockSpec((PROJ_ROWS, n), lambda i: (i, 0)),
        out_shape=jax.ShapeDtypeStruct((m, n), jnp.bfloat16),
        compiler_params=_params("parallel"),
        name="proj_a",
    )(x2, g.reshape(1, d), w)


def _split_bf16(x):
    hi = x.astype(jnp.bfloat16)
    lo = (x - hi.astype(jnp.float32)).astype(jnp.bfloat16)
    return hi, lo


def _head_norm_rope(acc, ng, cos, sin, seg_mean):
    width = seg_mean.shape[0]
    lane = lax.broadcasted_iota(jnp.int32, cos.shape, 1)
    first_half = (lane % HEAD_DIM) < (HEAD_DIM // 2)
    outs = []
    for c in range(acc.shape[1] // width):
        t = acc[:, c * width:(c + 1) * width]
        hi, lo = _split_bf16(t * t)
        ms = (jnp.dot(hi, seg_mean, preferred_element_type=jnp.float32)
              + jnp.dot(lo, seg_mean, preferred_element_type=jnp.float32))
        tn = t * lax.rsqrt(ms + EPS) * ng[:, c * width:(c + 1) * width]
        for s in range(width // LANES):
            u = tn[:, s * LANES:(s + 1) * LANES]
            swapped = jnp.where(first_half,
                                pltpu.roll(u, LANES - HEAD_DIM // 2, axis=1),
                                pltpu.roll(u, HEAD_DIM // 2, axis=1))
            outs.append(u * cos + swapped * sin)
    return jnp.concatenate(outs, axis=1)


def _gated_out_proj(gate_ref, attn_ref, x_ref, w_ref):
    gate = gate_ref[...].astype(jnp.float32)
    u = (gate * jax.nn.sigmoid(gate) * attn_ref[...].astype(jnp.float32)).astype(jnp.bfloat16)
    return x_ref[...] + jnp.dot(u, w_ref[...], preferred_element_type=jnp.float32)


def _proj_b_kernel(gate_ref, attn_ref, x_ref, wo_ref, g_ref, w_ref, ng_ref, cos_ref, sin_ref,
                   seg_ref, x1_ref, o_ref, vt_ref):
    x1 = _gated_out_proj(gate_ref, attn_ref, x_ref, wo_ref)
    x1_ref[...] = x1
    xn = _rms_scale(x1)
    for c in range(o_ref.shape[1] // PROJ_COLS):
        cols = slice(c * PROJ_COLS, (c + 1) * PROJ_COLS)
        h = (xn * g_ref[c // 2]).astype(jnp.bfloat16)
        acc = jnp.dot(h, w_ref[:, cols], preferred_element_type=jnp.float32)
        if c % 2 == 0:
            acc = _head_norm_rope(acc, ng_ref[c // 2], cos_ref[...], sin_ref[...], seg_ref[...])
        if c == 1:
            vt_ref[0] = acc.T.astype(vt_ref.dtype)
        if c == 2:
            acc = acc * DIFF_Q_SCALE
        o_ref[:, cols] = acc.astype(o_ref.dtype)


def _proj_b(prev_proj, prev_attn, x2, w_out, g2, w, ng2, cos, sin, batch):
    m, d = x2.shape
    n = w.shape[1]
    seq = m // batch
    tiles = seq // PROJ_ROWS
    width = 2 * LANES
    seg = np.arange(width)[:, None] // HEAD_DIM == np.arange(width)[None, :] // HEAD_DIM
    seg_mean = jnp.asarray(seg.astype(np.float32) / HEAD_DIM, jnp.bfloat16)
    return pl.pallas_call(
        _proj_b_kernel,
        grid=(m // PROJ_ROWS,),
        in_specs=[pl.BlockSpec((PROJ_ROWS, d), lambda i: (i, 3)),
                  pl.BlockSpec((PROJ_ROWS, d), lambda i: (i, 0)),
                  pl.BlockSpec((PROJ_ROWS, d), lambda i: (i, 0)),
                  pl.BlockSpec((d, d), lambda i: (0, 0)),
                  pl.BlockSpec((2, 1, d), lambda i: (0, 0, 0)),
                  pl.BlockSpec((d, n), lambda i: (0, 0)),
                  pl.BlockSpec((2, 1, PROJ_COLS), lambda i: (0, 0, 0)),
                  pl.BlockSpec((PROJ_ROWS, LANES), lambda i: (i, 0)),
                  pl.BlockSpec((PROJ_ROWS, LANES), lambda i: (i, 0)),
                  pl.BlockSpec((width, width), lambda i: (0, 0))],
        out_specs=[pl.BlockSpec((PROJ_ROWS, d), lambda i: (i, 0)),
                   pl.BlockSpec((PROJ_ROWS, n), lambda i: (i, 0)),
                   pl.BlockSpec((1, d, PROJ_ROWS), lambda i: (i // tiles, 0, i % tiles))],
        out_shape=[jax.ShapeDtypeStruct((m, d), jnp.float32),
                   jax.ShapeDtypeStruct((m, n), jnp.bfloat16),
                   jax.ShapeDtypeStruct((batch, d, seq), jnp.bfloat16)],
        compiler_params=_params("parallel"),
        name="proj_b",
    )(prev_proj, prev_attn, x2, w_out, g2, w, ng2, cos, sin, seg_mean)


def _out_proj_kernel(gate_ref, attn_ref, x_ref, w_ref, y_ref):
    y_ref[...] = _gated_out_proj(gate_ref, attn_ref, x_ref, w_ref)


def _out_proj(proj, gate_col_block, o, x2, w):
    m, d = x2.shape
    return pl.pallas_call(
        _out_proj_kernel,
        grid=(m // OUT_PROJ_ROWS,),
        in_specs=[pl.BlockSpec((OUT_PROJ_ROWS, d), lambda i: (i, gate_col_block)),
                  pl.BlockSpec((OUT_PROJ_ROWS, d), lambda i: (i, 0)),
                  pl.BlockSpec((OUT_PROJ_ROWS, d), lambda i: (i, 0)),
                  pl.BlockSpec((d, d), lambda i: (0, 0))],
        out_specs=pl.BlockSpec((OUT_PROJ_ROWS, d), lambda i: (i, 0)),
        out_shape=jax.ShapeDtypeStruct((m, d), jnp.float32),
        compiler_params=_params("parallel"),
        name="out_proj",
    )(proj, o, x2, w)


def _sb_kernel(q_ref, k_ref, v_ref, tri_ref, o_ref, acc_ref, c_ref, a_ref):
    tq = ATT_BLOCK
    i = pl.program_id(2)
    heads = range(SB_HEADS)
    gw = SB_GROUP * HEAD_DIM
    lane = lax.broadcasted_iota(jnp.int32, (1, gw), 1)

    def group(x, h):
        g = h // SB_GROUP
        return x[:, g * gw:(g + 1) * gw]

    qn = q_ref[0]
    qh = [jnp.where((lane // HEAD_DIM) == h % SB_GROUP, group(qn, h), 0).astype(qn.dtype)
          for h in heads]
    tri2 = tri_ref[...]

    def tile(ref, j):
        return ref[0, pl.ds(pl.multiple_of(j * tq, tq), tq), :]

    def scores(j):
        kb = tile(k_ref, j)
        return [lax.dot_general(qh[h], group(kb, h), _NT, preferred_element_type=jnp.float32)
                for h in heads]

    def weights(n, carry, strict):
        l = jnp.minimum(n, 0.0) - jnp.log2(1.0 + jnp.exp2(-jnp.abs(n)))
        if strict is not None:
            l = jnp.where(strict, l, 0.0)
        hi, lo = _split_bf16(l)
        incl = jnp.dot(jnp.concatenate([hi, lo], axis=1), tri2,
                       preferred_element_type=jnp.float32)
        a = jnp.exp2(incl - n + carry)
        if strict is not None:
            a = jnp.where(strict, a, 0.0)
        return a.astype(jnp.bfloat16), carry + incl[:, :1]

    def step(j, strict, first):
        n = scores(j)
        if not first:
            vb = tile(v_ref, j + 1)
            for h in heads:
                acc_ref[h] += jnp.dot(a_ref[h], group(vb, h),
                                      preferred_element_type=jnp.float32)
        carries = []
        for h in heads:
            carry = jnp.zeros((tq, 1), jnp.float32) if first else c_ref[h]
            a_ref[h], carry = weights(n[h], carry, strict)
            c_ref[h] = carry
            carries.append(carry)
        return jnp.max(functools.reduce(jnp.maximum, carries))

    acc_ref[...] = jnp.zeros_like(acc_ref)
    row = lax.broadcasted_iota(jnp.int32, (tq, tq), 0)
    col = lax.broadcasted_iota(jnp.int32, (tq, tq), 1)
    stick = step(i, col < row, True)

    def cond(state):
        t, stick = state
        return jnp.logical_and(t < i, stick > UNDERFLOW_LOG2)

    def body(state):
        t, _ = state
        return t + 1, step(i - 1 - t, None, False)

    done, _ = lax.while_loop(cond, body, (jnp.int32(0), stick))
    vb = tile(v_ref, i - done)
    out = [acc_ref[h] + jnp.dot(a_ref[h], group(vb, h), preferred_element_type=jnp.float32)
           for h in heads]
    res = []
    for g in range(SB_HEADS // SB_GROUP):
        r = out[g * SB_GROUP]
        for h in range(1, SB_GROUP):
            r = jnp.where((lane // HEAD_DIM) == h, out[g * SB_GROUP + h], r)
        res.append(r)
    o_ref[0] = jnp.concatenate(res, axis=1).astype(o_ref.dtype)


def _stick_breaking(qkvg, batch, seq):
    tq = ATT_BLOCK
    width = SB_HEADS * HEAD_DIM
    blocks = D_MODEL // width
    tri = (np.arange(tq)[:, None] >= np.arange(tq)[None, :]).astype(np.float32)
    tri2 = jnp.asarray(np.concatenate([tri, tri], axis=0), jnp.bfloat16)
    return pl.pallas_call(
        _sb_kernel,
        grid=(batch, blocks, seq // tq),
        in_specs=[pl.BlockSpec((1, tq, width), lambda b, h, i: (b, i, h)),
                  pl.BlockSpec((1, seq, width), lambda b, h, i: (b, 0, blocks + h)),
                  pl.BlockSpec((1, seq, width), lambda b, h, i: (b, 0, 2 * blocks + h)),
                  pl.BlockSpec((2 * tq, tq), lambda b, h, i: (0, 0))],
        out_specs=pl.BlockSpec((1, tq, width), lambda b, h, i: (b, i, h)),
        out_shape=jax.ShapeDtypeStruct((batch, seq, D_MODEL), jnp.bfloat16),
        scratch_shapes=[pltpu.VMEM((SB_HEADS, tq, SB_GROUP * HEAD_DIM), jnp.float32),
                        pltpu.VMEM((SB_HEADS, tq, 1), jnp.float32),
                        pltpu.VMEM((SB_HEADS, tq, tq), jnp.bfloat16)],
        compiler_params=_params("parallel", "parallel", "arbitrary"),
        name="stick_breaking",
    )(qkvg, qkvg, qkvg, tri2)


def _diff_kernel(q_ref, k_ref, vt_ref, lam_ref, sg_ref, o_ref, acc_ref, m_ref, alpha_ref, s_ref,
                 p_ref, *, lam_init):
    tq = tk = DIFF_BLOCK
    i = pl.program_id(2)
    pairs = tuple(range(2 * DIFF_HEADS))
    lane = lax.broadcasted_iota(jnp.int32, (1, LANES), 1)

    def head_lanes(c):
        return slice(c // 2 * LANES, (c // 2 + 1) * LANES)

    qs = q_ref[0]
    qc = [jnp.where((lane // HEAD_DIM) == c % 2, qs[:, head_lanes(c)], 0).astype(qs.dtype)
          for c in pairs]
    ones = jnp.ones((ONES_ROWS, tk), jnp.bfloat16)

    def scores(j, cs):
        keys = pl.ds(pl.multiple_of(j * tk, tk), tk)
        return [lax.dot_general(k_ref[0, keys, head_lanes(c)], qc[c], _NT,
                                preferred_element_type=jnp.float32) for c in cs]

    def accumulate(acc, alpha, p, j, cs):
        keys = pl.ds(pl.multiple_of(j * tk, tk), tk)
        out = []
        for n, c in enumerate(cs):
            vt1 = jnp.concatenate([vt_ref[0, head_lanes(c), keys], ones], axis=0)
            out.append(alpha[n] * acc[n] + jnp.dot(vt1, p[n], preferred_element_type=jnp.float32))
        return out

    def softmax_tile(s, m, causal):
        p, alpha, m_out = [], [], []
        for sc, m_old in zip(s, m):
            if causal is not None:
                sc = jnp.where(causal, sc, NEG_BIG)
            m_new = jnp.maximum(m_old, jnp.max(sc, axis=0, keepdims=True))
            alpha.append(jnp.exp2(m_old - m_new))
            p.append(jnp.exp2(sc - m_new).astype(jnp.bfloat16))
            m_out.append(m_new)
        return p, alpha, m_out

    def load(ref, cs):
        return [ref[c] for c in cs]

    def store(ref, vals, cs):
        for c, val in zip(cs, vals):
            ref[c] = val

    def step(j, cs):
        s_next = scores(jnp.maximum(j - 1, 0), cs)
        acc = accumulate(load(acc_ref, cs), load(alpha_ref, cs), load(p_ref, cs), j + 1, cs)
        p, alpha, m = softmax_tile(load(s_ref, cs), load(m_ref, cs), None)
        store(acc_ref, acc, cs), store(p_ref, p, cs), store(alpha_ref, alpha, cs)
        store(m_ref, m, cs), store(s_ref, s_next, cs)

    key = lax.broadcasted_iota(jnp.int32, (tk, tq), 0)
    qry = lax.broadcasted_iota(jnp.int32, (tk, tq), 1)
    s_diag = scores(i, pairs)
    s_next = scores(jnp.maximum(i - 1, 0), pairs)
    m0 = [jnp.full((1, tq), NEG_BIG, jnp.float32) for c in pairs]
    p, alpha, m = softmax_tile(s_diag, m0, key <= qry)
    store(p_ref, p, pairs), store(alpha_ref, alpha, pairs), store(m_ref, m, pairs)
    store(s_ref, s_next, pairs)
    acc_ref[...] = jnp.zeros_like(acc_ref)

    for h in range(DIFF_HEADS):
        def body(t, _, cs=(2 * h, 2 * h + 1)):
            step(i - 1 - t, cs)
            return 0

        lax.fori_loop(0, i, body, 0)
    acc = accumulate(load(acc_ref, pairs), load(alpha_ref, pairs), load(p_ref, pairs), 0, pairs)

    lp = lam_ref[...]
    lam = (jnp.exp(jnp.sum(lp[0:1] * lp[1:2], axis=-1, keepdims=True))
           - jnp.exp(jnp.sum(lp[2:3] * lp[3:4], axis=-1, keepdims=True)) + lam_init)
    num = [acc[c][:LANES] for c in pairs]
    den = [acc[c][LANES:LANES + 1] for c in pairs]
    outs = []
    for h in range(DIFF_HEADS):
        c1, c2 = 2 * h, 2 * h + 1
        ot = num[c1] / den[c1] - lam * (num[c2] / den[c2])
        ot = ot * lax.rsqrt(jnp.mean(ot * ot, axis=0, keepdims=True) + EPS)
        outs.append(ot.T * sg_ref[...] * (1.0 - lam_init))
    o_ref[0] = jnp.concatenate(outs, axis=1).astype(o_ref.dtype)


def _diff_attention(proj, vt, lam_params, subln_g, lam_init, batch, seq):
    tq = DIFF_BLOCK
    width = DIFF_HEADS * LANES
    blocks = D_MODEL // width
    rows = LANES + ONES_ROWS
    pairs = 2 * DIFF_HEADS
    return pl.pallas_call(
        functools.partial(_diff_kernel, lam_init=lam_init),
        grid=(batch, blocks, seq // tq),
        in_specs=[pl.BlockSpec((1, tq, width), lambda b, h, i: (b, i, 2 * blocks + h)),
                  pl.BlockSpec((1, seq, width), lambda b, h, i: (b, 0, h)),
                  pl.BlockSpec((1, width, seq), lambda b, h, i: (b, h, 0)),
                  pl.BlockSpec((4, HEAD_DIM), lambda b, h, i: (0, 0)),
                  pl.BlockSpec((1, LANES), lambda b, h, i: (0, 0))],
        out_specs=pl.BlockSpec((1, tq, width), lambda b, h, i: (b, i, h)),
        out_shape=jax.ShapeDtypeStruct((batch, seq, D_MODEL), jnp.bfloat16),
        scratch_shapes=[pltpu.VMEM((pairs, rows, tq), jnp.float32),
                        pltpu.VMEM((pairs, 1, tq), jnp.float32),
                        pltpu.VMEM((pairs, 1, tq), jnp.float32),
                        pltpu.VMEM((pairs, tq, tq), jnp.float32),
                        pltpu.VMEM((pairs, tq, tq), jnp.bfloat16)],
        compiler_params=_params("parallel", "parallel", "arbitrary"),
        name="diff_attention",
    )(proj, proj, vt, lam_params, subln_g.reshape(1, LANES))


def kernel(x, positions, a_norm_g, a_w_in, a_w_out, kv_norm_g, w_kv, k_norm_g, b_norm_g,
           b_w_in, b_q_norm_g, b_lambda, b_subln_g, b_w_out):
    batch, seq, d = x.shape
    assert d == D_MODEL and seq % ATT_BLOCK == 0 and seq % DIFF_BLOCK == 0
    assert seq % PROJ_ROWS == 0 and (batch * seq) % OUT_PROJ_ROWS == 0
    assert a_norm_g.shape[0] == 1 and b_norm_g.shape[0] == 1
    bf16 = jnp.bfloat16
    x2 = x.reshape(batch * seq, d)

    qkvg = _proj_a(x2, a_norm_g[0], a_w_in[0].astype(bf16))
    o = _stick_breaking(qkvg.reshape(batch, seq, 4 * d), batch, seq)

    cos, sin = _rope_tables(positions)
    w_b = jnp.concatenate([w_kv, b_w_in[0]], axis=1).astype(bf16)
    g2 = jnp.stack([kv_norm_g, b_norm_g[0]]).reshape(2, 1, d)
    ng2 = jnp.stack([jnp.tile(k_norm_g, d // HEAD_DIM),
                     jnp.tile(b_q_norm_g[0], d // HEAD_DIM)]).reshape(2, 1, d)
    x2, proj, vt = _proj_b(qkvg, o.reshape(batch * seq, d), x2, a_w_out[0].astype(bf16), g2, w_b,
                           ng2, cos, sin, batch)
    layer_idx = N_A_LAYERS
    lam_init = 0.8 - 0.6 * math.exp(-0.3 * layer_idx)
    o = _diff_attention(proj.reshape(batch, seq, 4 * d), vt, b_lambda[0], b_subln_g[0],
                        lam_init, batch, seq)
    x2 = _out_proj(proj, 3, o.reshape(batch * seq, d), x2, b_w_out[0].astype(bf16))
    return x2.reshape(batch, seq, d)
```

```python
import functools
import math

import jax
import jax.numpy as jnp
import numpy as np
from jax import lax
from jax.experimental import pallas as pl
from jax.experimental.pallas import tpu as pltpu

D_MODEL = 1024
HEAD_DIM = 64
LANES = 128
ROPE_THETA = 10000.0
EPS = 1e-6
DEPTH = 2
N_A_LAYERS = DEPTH // 2
VMEM_LIMIT_BYTES = 56 * 1024 * 1024

PROJ_ROWS = 512
OUT_PROJ_ROWS = 1024
PROJ_COLS = 1024
ATT_BLOCK = 256
SB_GROUP = 4
SB_HEADS = 8
DIFF_BLOCK = 512
DIFF_HEADS = 4
ONES_ROWS = 16
NEG_BIG = -1e30
LOG2E = math.log2(math.e)
DIFF_Q_SCALE = LOG2E / math.sqrt(HEAD_DIM)
SB_Q_SCALE = -LOG2E / math.sqrt(HEAD_DIM)
UNDERFLOW_LOG2 = -150.0

_NT = (((1,), (1,)), ((), ()))


def _params(*sem):
    return pltpu.CompilerParams(dimension_semantics=sem, vmem_limit_bytes=VMEM_LIMIT_BYTES)


def _rope_cos_sin(pos, freq):
    ang = pos.astype(jnp.float32) * freq
    lane = lax.broadcasted_iota(jnp.int32, ang.shape, 1)
    first_half = (lane % HEAD_DIM) < (HEAD_DIM // 2)
    sin = jnp.sin(ang)
    return jnp.cos(ang), jnp.where(first_half, -sin, sin)


def _rms_scale(x):
    return x * lax.rsqrt(jnp.mean(x * x, axis=-1, keepdims=True) + EPS)


def _proj_a_kernel(x_ref, g_ref, w_ref, o_ref):
    h = (_rms_scale(x_ref[...]) * g_ref[...]).astype(jnp.bfloat16)
    for c in range(o_ref.shape[1] // PROJ_COLS):
        cols = slice(c * PROJ_COLS, (c + 1) * PROJ_COLS)
        acc = jnp.dot(h, w_ref[:, cols], preferred_element_type=jnp.float32)
        if c == 0:
            acc = acc * SB_Q_SCALE
        o_ref[:, cols] = acc.astype(o_ref.dtype)


def _proj_a(x2, g, w):
    m, d = x2.shape
    n = w.shape[1]
    return pl.pallas_call(
        _proj_a_kernel,
        grid=(m // PROJ_ROWS,),
        in_specs=[pl.BlockSpec((PROJ_ROWS, d), lambda i: (i, 0)),
                  pl.BlockSpec((1, d), lambda i: (0, 0)),
                  pl.BlockSpec((d, n), lambda i: (0, 0))],
        out_specs=pl.BlockSpec((PROJ_ROWS, n), lambda i: (i, 0)),
        out_shape=jax.ShapeDtypeStruct((m, n), jnp.bfloat16),
        compiler_params=_params("parallel"),
        name="proj_a",
    )(x2, g.reshape(1, d), w)


def _split_bf16(x):
    hi = x.astype(jnp.bfloat16)
    lo = (x - hi.astype(jnp.float32)).astype(jnp.bfloat16)
    return hi, lo


def _head_norm_rope(acc, ng, cos, sin, seg_mean):
    width = seg_mean.shape[0]
    lane = lax.broadcasted_iota(jnp.int32, cos.shape, 1)
    first_half = (lane % HEAD_DIM) < (HEAD_DIM // 2)
    outs = []
    for c in range(acc.shape[1] // width):
        t = acc[:, c * width:(c + 1) * width]
        hi, lo = _split_bf16(t * t)
        ms = (jnp.dot(hi, seg_mean, preferred_element_type=jnp.float32)
              + jnp.dot(lo, seg_mean, preferred_element_type=jnp.float32))
        tn = t * lax.rsqrt(ms + EPS) * ng[:, c * width:(c + 1) * width]
        for s in range(width // LANES):
            u = tn[:, s * LANES:(s + 1) * LANES]
            swapped = jnp.where(first_half,
                                pltpu.roll(u, LANES - HEAD_DIM // 2, axis=1),
                                pltpu.roll(u, HEAD_DIM // 2, axis=1))
            outs.append(u * cos + swapped * sin)
    return jnp.concatenate(outs, axis=1)


def _gated_out_proj(gate_ref, attn_ref, x_ref, w_ref):
    gate = gate_ref[...].astype(jnp.float32)
    u = (gate * jax.nn.sigmoid(gate) * attn_ref[...].astype(jnp.float32)).astype(jnp.bfloat16)
    return x_ref[...] + jnp.dot(u, w_ref[...], preferred_element_type=jnp.float32)


def _proj_b_kernel(gate_ref, attn_ref, x_ref, wo_ref, g_ref, w_ref, ng_ref, pos_ref, freq_ref,
                   seg_ref, x1_ref, o_ref, vt_ref):
    x1 = _gated_out_proj(gate_ref, attn_ref, x_ref, wo_ref)
    x1_ref[...] = x1
    xn = _rms_scale(x1)
    cos, sin = _rope_cos_sin(pos_ref[...], freq_ref[...])
    for c in range(o_ref.shape[1] // PROJ_COLS):
        cols = slice(c * PROJ_COLS, (c + 1) * PROJ_COLS)
        h = (xn * g_ref[c // 2]).astype(jnp.bfloat16)
        acc = jnp.dot(h, w_ref[:, cols], preferred_element_type=jnp.float32)
        if c % 2 == 0:
            acc = _head_norm_rope(acc, ng_ref[c // 2], cos, sin, seg_ref[...])
        if c == 1:
            vt_ref[0] = acc.T.astype(vt_ref.dtype)
        if c == 2:
            acc = acc * DIFF_Q_SCALE
        o_ref[:, cols] = acc.astype(o_ref.dtype)


def _proj_b(prev_proj, prev_attn, x2, w_out, g2, w, ng2, positions, batch):
    m, d = x2.shape
    inv_freq = ROPE_THETA ** (-jnp.arange(0, HEAD_DIM, 2, dtype=jnp.float32) / HEAD_DIM)
    freq = jnp.tile(inv_freq, LANES // (HEAD_DIM // 2)).reshape(1, LANES)
    n = w.shape[1]
    seq = m // batch
    tiles = seq // PROJ_ROWS
    width = 2 * LANES
    seg = np.arange(width)[:, None] // HEAD_DIM == np.arange(width)[None, :] // HEAD_DIM
    seg_mean = jnp.asarray(seg.astype(np.float32) / HEAD_DIM, jnp.bfloat16)
    return pl.pallas_call(
        _proj_b_kernel,
        grid=(m // PROJ_ROWS,),
        in_specs=[pl.BlockSpec((PROJ_ROWS, d), lambda i: (i, 3)),
                  pl.BlockSpec((PROJ_ROWS, d), lambda i: (i, 0)),
                  pl.BlockSpec((PROJ_ROWS, d), lambda i: (i, 0)),
                  pl.BlockSpec((d, d), lambda i: (0, 0)),
                  pl.BlockSpec((2, 1, d), lambda i: (0, 0, 0)),
                  pl.BlockSpec((d, n), lambda i: (0, 0)),
                  pl.BlockSpec((2, 1, PROJ_COLS), lambda i: (0, 0, 0)),
                  pl.BlockSpec((PROJ_ROWS, 1), lambda i: (i, 0)),
                  pl.BlockSpec((1, LANES), lambda i: (0, 0)),
                  pl.BlockSpec((width, width), lambda i: (0, 0))],
        out_specs=[pl.BlockSpec((PROJ_ROWS, d), lambda i: (i, 0)),
                   pl.BlockSpec((PROJ_ROWS, n), lambda i: (i, 0)),
                   pl.BlockSpec((1, d, PROJ_ROWS), lambda i: (i // tiles, 0, i % tiles))],
        out_shape=[jax.ShapeDtypeStruct((m, d), jnp.float32),
                   jax.ShapeDtypeStruct((m, n), jnp.bfloat16),
                   jax.ShapeDtypeStruct((batch, d, seq), jnp.bfloat16)],
        compiler_params=_params("parallel"),
        name="proj_b",
    )(prev_proj, prev_attn, x2, w_out, g2, w, ng2, positions.reshape(m, 1), freq, seg_mean)


def _out_proj_kernel(gate_ref, attn_ref, x_ref, w_ref, y_ref):
    y_ref[...] = _gated_out_proj(gate_ref, attn_ref, x_ref, w_ref)


def _out_proj(proj, gate_col_block, o, x2, w):
    m, d = x2.shape
    return pl.pallas_call(
        _out_proj_kernel,
        grid=(m // OUT_PROJ_ROWS,),
        in_specs=[pl.BlockSpec((OUT_PROJ_ROWS, d), lambda i: (i, gate_col_block)),
                  pl.BlockSpec((OUT_PROJ_ROWS, d), lambda i: (i, 0)),
                  pl.BlockSpec((OUT_PROJ_ROWS, d), lambda i: (i, 0)),
                  pl.BlockSpec((d, d), lambda i: (0, 0))],
        out_specs=pl.BlockSpec((OUT_PROJ_ROWS, d), lambda i: (i, 0)),
        out_shape=jax.ShapeDtypeStruct((m, d), jnp.float32),
        compiler_params=_params("parallel"),
        name="out_proj",
    )(proj, o, x2, w)


def _sb_kernel(q_ref, k_ref, v_ref, tri_ref, o_ref, acc_ref, c_ref, a_ref):
    tq = ATT_BLOCK
    i = pl.program_id(2)
    heads = range(SB_HEADS)
    gw = SB_GROUP * HEAD_DIM
    lane = lax.broadcasted_iota(jnp.int32, (1, gw), 1)

    def group(x, h):
        g = h // SB_GROUP
        return x[:, g * gw:(g + 1) * gw]

    qn = q_ref[0]
    qh = [jnp.where((lane // HEAD_DIM) == h % SB_GROUP, group(qn, h), 0).astype(qn.dtype)
          for h in heads]
    tri2 = tri_ref[...]

    def tile(ref, j):
        return ref[0, pl.ds(pl.multiple_of(j * tq, tq), tq), :]

    def scores(j):
        kb = tile(k_ref, j)
        return [lax.dot_general(qh[h], group(kb, h), _NT, preferred_element_type=jnp.float32)
                for h in heads]

    def weights(n, carry, strict):
        l = jnp.minimum(n, 0.0) - jnp.log2(1.0 + jnp.exp2(-jnp.abs(n)))
        if strict is not None:
            l = jnp.where(strict, l, 0.0)
        hi, lo = _split_bf16(l)
        incl = jnp.dot(jnp.concatenate([hi, lo], axis=1), tri2,
                       preferred_element_type=jnp.float32)
        a = jnp.exp2(incl - n + carry)
        if strict is not None:
            a = jnp.where(strict, a, 0.0)
        return a.astype(jnp.bfloat16), carry + incl[:, :1]

    def step(j, strict, first):
        n = scores(j)
        if not first:
            vb = tile(v_ref, j + 1)
            for h in heads:
                acc_ref[h] += jnp.dot(a_ref[h], group(vb, h),
                                      preferred_element_type=jnp.float32)
        carries = []
        for h in heads:
            carry = jnp.zeros((tq, 1), jnp.float32) if first else c_ref[h]
            a_ref[h], carry = weights(n[h], carry, strict)
            c_ref[h] = carry
            carries.append(carry)
        return jnp.max(functools.reduce(jnp.maximum, carries))

    acc_ref[...] = jnp.zeros_like(acc_ref)
    row = lax.broadcasted_iota(jnp.int32, (tq, tq), 0)
    col = lax.broadcasted_iota(jnp.int32, (tq, tq), 1)
    stick = step(i, col < row, True)

    def cond(state):
        t, stick = state
        return jnp.logical_and(t < i, stick > UNDERFLOW_LOG2)

    def body(state):
        t, _ = state
        return t + 1, step(i - 1 - t, None, False)

    done, _ = lax.while_loop(cond, body, (jnp.int32(0), stick))
    vb = tile(v_ref, i - done)
    out = [acc_ref[h] + jnp.dot(a_ref[h], group(vb, h), preferred_element_type=jnp.float32)
           for h in heads]
    res = []
    for g in range(SB_HEADS // SB_GROUP):
        r = out[g * SB_GROUP]
        for h in range(1, SB_GROUP):
            r = jnp.where((lane // HEAD_DIM) == h, out[g * SB_GROUP + h], r)
        res.append(r)
    o_ref[0] = jnp.concatenate(res, axis=1).astype(o_ref.dtype)


def _stick_breaking(qkvg, batch, seq):
    tq = ATT_BLOCK
    width = SB_HEADS * HEAD_DIM
    blocks = D_MODEL // width
    tri = (np.arange(tq)[:, None] >= np.arange(tq)[None, :]).astype(np.float32)
    tri2 = jnp.asarray(np.concatenate([tri, tri], axis=0), jnp.bfloat16)
    return pl.pallas_call(
        _sb_kernel,
        grid=(batch, blocks, seq // tq),
        in_specs=[pl.BlockSpec((1, tq, width), lambda b, h, i: (b, i, h)),
                  pl.BlockSpec((1, seq, width), lambda b, h, i: (b, 0, blocks + h)),
                  pl.BlockSpec((1, seq, width), lambda b, h, i: (b, 0, 2 * blocks + h)),
                  pl.BlockSpec((2 * tq, tq), lambda b, h, i: (0, 0))],
        out_specs=pl.BlockSpec((1, tq, width), lambda b, h, i: (b, i, h)),
        out_shape=jax.ShapeDtypeStruct((batch, seq, D_MODEL), jnp.bfloat16),
        scratch_shapes=[pltpu.VMEM((SB_HEADS, tq, SB_GROUP * HEAD_DIM), jnp.float32),
                        pltpu.VMEM((SB_HEADS, tq, 1), jnp.float32),
                        pltpu.VMEM((SB_HEADS, tq, tq), jnp.bfloat16)],
        compiler_params=_params("parallel", "parallel", "arbitrary"),
        name="stick_breaking",
    )(qkvg, qkvg, qkvg, tri2)


def _diff_kernel(q_ref, k_ref, vt_ref, lam_ref, sg_ref, o_ref, acc_ref, m_ref, alpha_ref, s_ref,
                 p_ref, *, lam_init):
    tq = tk = DIFF_BLOCK
    i = pl.program_id(2)
    pairs = tuple(range(2 * DIFF_HEADS))
    lane = lax.broadcasted_iota(jnp.int32, (1, LANES), 1)

    def head_lanes(c):
        return slice(c // 2 * LANES, (c // 2 + 1) * LANES)

    qs = q_ref[0]
    qc = [jnp.where((lane // HEAD_DIM) == c % 2, qs[:, head_lanes(c)], 0).astype(qs.dtype)
          for c in pairs]
    ones = jnp.ones((ONES_ROWS, tk), jnp.bfloat16)

    def scores(j, cs):
        keys = pl.ds(pl.multiple_of(j * tk, tk), tk)
        return [lax.dot_general(k_ref[0, keys, head_lanes(c)], qc[c], _NT,
                                preferred_element_type=jnp.float32) for c in cs]

    def accumulate(acc, alpha, p, j, cs):
        keys = pl.ds(pl.multiple_of(j * tk, tk), tk)
        out = []
        for n, c in enumerate(cs):
            vt1 = jnp.concatenate([vt_ref[0, head_lanes(c), keys], ones], axis=0)
            out.append(alpha[n] * acc[n] + jnp.dot(vt1, p[n], preferred_element_type=jnp.float32))
        return out

    def softmax_tile(s, m, causal):
        p, alpha, m_out = [], [], []
        for sc, m_old in zip(s, m):
            if causal is not None:
                sc = jnp.where(causal, sc, NEG_BIG)
            m_new = jnp.maximum(m_old, jnp.max(sc, axis=0, keepdims=True))
            alpha.append(jnp.exp2(m_old - m_new))
            p.append(jnp.exp2(sc - m_new).astype(jnp.bfloat16))
            m_out.append(m_new)
        return p, alpha, m_out

    def load(ref, cs):
        return [ref[c] for c in cs]

    def store(ref, vals, cs):
        for c, val in zip(cs, vals):
            ref[c] = val

    def step(j, cs):
        s_next = scores(jnp.maximum(j - 1, 0), cs)
        acc = accumulate(load(acc_ref, cs), load(alpha_ref, cs), load(p_ref, cs), j + 1, cs)
        p, alpha, m = softmax_tile(load(s_ref, cs), load(m_ref, cs), None)
        store(acc_ref, acc, cs), store(p_ref, p, cs), store(alpha_ref, alpha, cs)
        store(m_ref, m, cs), store(s_ref, s_next, cs)

    key = lax.broadcasted_iota(jnp.int32, (tk, tq), 0)
    qry = lax.broadcasted_iota(jnp.int32, (tk, tq), 1)
    s_diag = scores(i, pairs)
    s_next = scores(jnp.maximum(i - 1, 0), pairs)
    m0 = [jnp.full((1, tq), NEG_BIG, jnp.float32) for c in pairs]
    p, alpha, m = softmax_tile(s_diag, m0, key <= qry)
    store(p_ref, p, pairs), store(alpha_ref, alpha, pairs), store(m_ref, m, pairs)
    store(s_ref, s_next, pairs)
    acc_ref[...] = jnp.zeros_like(acc_ref)

    for h in range(DIFF_HEADS):
        def body(t, _, cs=(2 * h, 2 * h + 1)):
            step(i - 1 - t, cs)
            return 0

        lax.fori_loop(0, i, body, 0)
    acc = accumulate(load(acc_ref, pairs), load(alpha_ref, pairs), load(p_ref, pairs), 0, pairs)

    lp = lam_ref[...]
    lam = (jnp.exp(jnp.sum(lp[0:1] * lp[1:2], axis=-1, keepdims=True))
           - jnp.exp(jnp.sum(lp[2:3] * lp[3:4], axis=-1, keepdims=True)) + lam_init)
    num = [acc[c][:LANES] for c in pairs]
    den = [acc[c][LANES:LANES + 1] for c in pairs]
    outs = []
    for h in range(DIFF_HEADS):
        c1, c2 = 2 * h, 2 * h + 1
        ot = num[c1] / den[c1] - lam * (num[c2] / den[c2])
        ot = ot * lax.rsqrt(jnp.mean(ot * ot, axis=0, keepdims=True) + EPS)
        outs.append(ot.T * sg_ref[...] * (1.0 - lam_init))
    o_ref[0] = jnp.concatenate(outs, axis=1).astype(o_ref.dtype)


def _diff_attention(proj, vt, lam_params, subln_g, lam_init, batch, seq):
    tq = DIFF_BLOCK
    width = DIFF_HEADS * LANES
    blocks = D_MODEL // width
    rows = LANES + ONES_ROWS
    pairs = 2 * DIFF_HEADS
    return pl.pallas_call(
        functools.partial(_diff_kernel, lam_init=lam_init),
        grid=(batch, blocks, seq // tq),
        in_specs=[pl.BlockSpec((1, tq, width), lambda b, h, i: (b, i, 2 * blocks + h)),
                  pl.BlockSpec((1, seq, width), lambda b, h, i: (b, 0, h)),
                  pl.BlockSpec((1, width, seq), lambda b, h, i: (b, h, 0)),
                  pl.BlockSpec((4, HEAD_DIM), lambda b, h, i: (0, 0)),
                  pl.BlockSpec((1, LANES), lambda b, h, i: (0, 0))],
        out_specs=pl.BlockSpec((1, tq, width), lambda b, h, i: (b, i, h)),
        out_shape=jax.ShapeDtypeStruct((batch, seq, D_MODEL), jnp.bfloat16),
        scratch_shapes=[pltpu.VMEM((pairs, rows, tq), jnp.float32),
                        pltpu.VMEM((pairs, 1, tq), jnp.float32),
                        pltpu.VMEM((pairs, 1, tq), jnp.float32),
                        pltpu.VMEM((pairs, tq, tq), jnp.float32),
                        pltpu.VMEM((pairs, tq, tq), jnp.bfloat16)],
        compiler_params=_params("parallel", "parallel", "arbitrary"),
        name="diff_attention",
    )(proj, proj, vt, lam_params, subln_g.reshape(1, LANES))


def kernel(x, positions, a_norm_g, a_w_in, a_w_out, kv_norm_g, w_kv, k_norm_g, b_norm_g,
           b_w_in, b_q_norm_g, b_lambda, b_subln_g, b_w_out):
    batch, seq, d = x.shape
    assert d == D_MODEL and seq % ATT_BLOCK == 0 and seq % DIFF_BLOCK == 0
    assert seq % PROJ_ROWS == 0 and (batch * seq) % OUT_PROJ_ROWS == 0
    assert a_norm_g.shape[0] == 1 and b_norm_g.shape[0] == 1
    bf16 = jnp.bfloat16
    x2 = x.reshape(batch * seq, d)

    qkvg = _proj_a(x2, a_norm_g[0], a_w_in[0].astype(bf16))
    o = _stick_breaking(qkvg.reshape(batch, seq, 4 * d), batch, seq)

    w_b = jnp.concatenate([w_kv, b_w_in[0]], axis=1).astype(bf16)
    g2 = jnp.stack([kv_norm_g, b_norm_g[0]]).reshape(2, 1, d)
    ng2 = jnp.stack([jnp.tile(k_norm_g, d // HEAD_DIM),
                     jnp.tile(b_q_norm_g[0], d // HEAD_DIM)]).reshape(2, 1, d)
    x2, proj, vt = _proj_b(qkvg, o.reshape(batch * seq, d), x2, a_w_out[0].astype(bf16), g2, w_b,
                           ng2, positions, batch)
    layer_idx = N_A_LAYERS
    lam_init = 0.8 - 0.6 * math.exp(-0.3 * layer_idx)
    o = _diff_attention(proj.reshape(batch, seq, 4 * d), vt, b_lambda[0], b_subln_g[0],
                        lam_init, batch, seq)
    x2 = _out_proj(proj, 3, o.reshape(batch * seq, d), x2, b_w_out[0].astype(bf16))
    return x2.reshape(batch, seq, d)
```

```python
import functools
import math

import jax
import jax.numpy as jnp
import numpy as np
from jax import lax
from jax.experimental import pallas as pl
from jax.experimental.pallas import tpu as pltpu

D_MODEL = 1024
HEAD_DIM = 64
LANES = 128
ROPE_THETA = 10000.0
EPS = 1e-6
DEPTH = 2
N_A_LAYERS = DEPTH // 2
VMEM_LIMIT_BYTES = 56 * 1024 * 1024

PROJ_ROWS = 512
OUT_PROJ_ROWS = 1024
PROJ_COLS = 1024
ATT_BLOCK = 256
SB_GROUP = 4
SB_HEADS = 8
DIFF_BLOCK = 512
DIFF_HEADS = 4
ONES_ROWS = 16
NEG_BIG = -1e30
LOG2E = math.log2(math.e)
DIFF_Q_SCALE = LOG2E / math.sqrt(HEAD_DIM)
SB_Q_SCALE = LOG2E / math.sqrt(HEAD_DIM)
UNDERFLOW_LOG2 = -150.0
SB_SCORE_CLAMP = 126.0

_NT = (((1,), (1,)), ((), ()))


def _params(*sem):
    return pltpu.CompilerParams(dimension_semantics=sem, vmem_limit_bytes=VMEM_LIMIT_BYTES)


def _rope_cos_sin(pos, freq):
    ang = pos.astype(jnp.float32) * freq
    lane = lax.broadcasted_iota(jnp.int32, ang.shape, 1)
    first_half = (lane % HEAD_DIM) < (HEAD_DIM // 2)
    sin = jnp.sin(ang)
    return jnp.cos(ang), jnp.where(first_half, -sin, sin)


def _rms_scale(x):
    return x * lax.rsqrt(jnp.mean(x * x, axis=-1, keepdims=True) + EPS)


def _proj_a_kernel(x_ref, g_ref, w_ref, o_ref):
    h = (_rms_scale(x_ref[...]) * g_ref[...]).astype(jnp.bfloat16)
    for c in range(o_ref.shape[1] // PROJ_COLS):
        cols = slice(c * PROJ_COLS, (c + 1) * PROJ_COLS)
        acc = jnp.dot(h, w_ref[:, cols], preferred_element_type=jnp.float32)
        if c == 0:
            acc = acc * SB_Q_SCALE
        o_ref[:, cols] = acc.astype(o_ref.dtype)


def _proj_a(x2, g, w):
    m, d = x2.shape
    n = w.shape[1]
    return pl.pallas_call(
        _proj_a_kernel,
        grid=(m // PROJ_ROWS,),
        in_specs=[pl.BlockSpec((PROJ_ROWS, d), lambda i: (i, 0)),
                  pl.BlockSpec((1, d), lambda i: (0, 0)),
                  pl.BlockSpec((d, n), lambda i: (0, 0))],
        out_specs=pl.BlockSpec((PROJ_ROWS, n), lambda i: (i, 0)),
        out_shape=jax.ShapeDtypeStruct((m, n), jnp.bfloat16),
        compiler_params=_params("parallel"),
        name="proj_a",
    )(x2, g.reshape(1, d), w)


def _split_bf16(x):
    hi = x.astype(jnp.bfloat16)
    lo = (x - hi.astype(jnp.float32)).astype(jnp.bfloat16)
    return hi, lo


def _head_norm_rope(acc, ng, cos, sin, seg_mean):
    width = seg_mean.shape[0]
    lane = lax.broadcasted_iota(jnp.int32, cos.shape, 1)
    first_half = (lane % HEAD_DIM) < (HEAD_DIM // 2)
    outs = []
    for c in range(acc.shape[1] // width):
        t = acc[:, c * width:(c + 1) * width]
        hi, lo = _split_bf16(t * t)
        ms = (jnp.dot(hi, seg_mean, preferred_element_type=jnp.float32)
              + jnp.dot(lo, seg_mean, preferred_element_type=jnp.float32))
        tn = t * lax.rsqrt(ms + EPS) * ng[:, c * width:(c + 1) * width]
        for s in range(width // LANES):
            u = tn[:, s * LANES:(s + 1) * LANES]
            swapped = jnp.where(first_half,
                                pltpu.roll(u, LANES - HEAD_DIM // 2, axis=1),
                                pltpu.roll(u, HEAD_DIM // 2, axis=1))
            outs.append(u * cos + swapped * sin)
    return jnp.concatenate(outs, axis=1)


def _gated_out_proj(gate_ref, attn_ref, x_ref, w_ref):
    gate = gate_ref[...].astype(jnp.float32)
    u = (gate * jax.nn.sigmoid(gate) * attn_ref[...].astype(jnp.float32)).astype(jnp.bfloat16)
    return x_ref[...] + jnp.dot(u, w_ref[...], preferred_element_type=jnp.float32)


def _proj_b_kernel(gate_ref, attn_ref, x_ref, wo_ref, g_ref, w_ref, ng_ref, pos_ref, freq_ref,
                   seg_ref, x1_ref, o_ref, vt_ref):
    x1 = _gated_out_proj(gate_ref, attn_ref, x_ref, wo_ref)
    x1_ref[...] = x1
    xn = _rms_scale(x1)
    cos, sin = _rope_cos_sin(pos_ref[...], freq_ref[...])
    for c in range(o_ref.shape[1] // PROJ_COLS):
        cols = slice(c * PROJ_COLS, (c + 1) * PROJ_COLS)
        h = (xn * g_ref[c // 2]).astype(jnp.bfloat16)
        acc = jnp.dot(h, w_ref[:, cols], preferred_element_type=jnp.float32)
        if c % 2 == 0:
            acc = _head_norm_rope(acc, ng_ref[c // 2], cos, sin, seg_ref[...])
        if c == 1:
            vt_ref[0] = acc.T.astype(vt_ref.dtype)
        if c == 2:
            acc = acc * DIFF_Q_SCALE
        o_ref[:, cols] = acc.astype(o_ref.dtype)


def _proj_b(prev_proj, prev_attn, x2, w_out, g2, w, ng2, positions, batch):
    m, d = x2.shape
    inv_freq = ROPE_THETA ** (-jnp.arange(0, HEAD_DIM, 2, dtype=jnp.float32) / HEAD_DIM)
    freq = jnp.tile(inv_freq, LANES // (HEAD_DIM // 2)).reshape(1, LANES)
    n = w.shape[1]
    seq = m // batch
    tiles = seq // PROJ_ROWS
    width = 2 * LANES
    seg = np.arange(width)[:, None] // HEAD_DIM == np.arange(width)[None, :] // HEAD_DIM
    seg_mean = jnp.asarray(seg.astype(np.float32) / HEAD_DIM, jnp.bfloat16)
    return pl.pallas_call(
        _proj_b_kernel,
        grid=(m // PROJ_ROWS,),
        in_specs=[pl.BlockSpec((PROJ_ROWS, d), lambda i: (i, 3)),
                  pl.BlockSpec((PROJ_ROWS, d), lambda i: (i, 0)),
                  pl.BlockSpec((PROJ_ROWS, d), lambda i: (i, 0)),
                  pl.BlockSpec((d, d), lambda i: (0, 0)),
                  pl.BlockSpec((2, 1, d), lambda i: (0, 0, 0)),
                  pl.BlockSpec((d, n), lambda i: (0, 0)),
                  pl.BlockSpec((2, 1, PROJ_COLS), lambda i: (0, 0, 0)),
                  pl.BlockSpec((PROJ_ROWS, 1), lambda i: (i, 0)),
                  pl.BlockSpec((1, LANES), lambda i: (0, 0)),
                  pl.BlockSpec((width, width), lambda i: (0, 0))],
        out_specs=[pl.BlockSpec((PROJ_ROWS, d), lambda i: (i, 0)),
                   pl.BlockSpec((PROJ_ROWS, n), lambda i: (i, 0)),
                   pl.BlockSpec((1, d, PROJ_ROWS), lambda i: (i // tiles, 0, i % tiles))],
        out_shape=[jax.ShapeDtypeStruct((m, d), jnp.float32),
                   jax.ShapeDtypeStruct((m, n), jnp.bfloat16),
                   jax.ShapeDtypeStruct((batch, d, seq), jnp.bfloat16)],
        compiler_params=_params("parallel"),
        name="proj_b",
    )(prev_proj, prev_attn, x2, w_out, g2, w, ng2, positions.reshape(m, 1), freq, seg_mean)


def _out_proj_kernel(gate_ref, attn_ref, x_ref, w_ref, y_ref):
    y_ref[...] = _gated_out_proj(gate_ref, attn_ref, x_ref, w_ref)


def _out_proj(proj, gate_col_block, o, x2, w):
    m, d = x2.shape
    return pl.pallas_call(
        _out_proj_kernel,
        grid=(m // OUT_PROJ_ROWS,),
        in_specs=[pl.BlockSpec((OUT_PROJ_ROWS, d), lambda i: (i, gate_col_block)),
                  pl.BlockSpec((OUT_PROJ_ROWS, d), lambda i: (i, 0)),
                  pl.BlockSpec((OUT_PROJ_ROWS, d), lambda i: (i, 0)),
                  pl.BlockSpec((d, d), lambda i: (0, 0))],
        out_specs=pl.BlockSpec((OUT_PROJ_ROWS, d), lambda i: (i, 0)),
        out_shape=jax.ShapeDtypeStruct((m, d), jnp.float32),
        compiler_params=_params("parallel"),
        name="out_proj",
    )(proj, o, x2, w)


def _sb_kernel(q_ref, k_ref, v_ref, tri_ref, o_ref, acc_ref, c_ref, a_ref):
    tq = ATT_BLOCK
    i = pl.program_id(2)
    heads = range(SB_HEADS)
    gw = SB_GROUP * HEAD_DIM
    lane = lax.broadcasted_iota(jnp.int32, (1, gw), 1)

    def group(x, h):
        g = h // SB_GROUP
        return x[:, g * gw:(g + 1) * gw]

    qn = q_ref[0]
    qh = [jnp.where((lane // HEAD_DIM) == h % SB_GROUP, group(qn, h), 0).astype(qn.dtype)
          for h in heads]
    tri2 = tri_ref[...]

    def tile(ref, j):
        return ref[0, pl.ds(pl.multiple_of(j * tq, tq), tq), :]

    def scores(j):
        kb = tile(k_ref, j)
        return [lax.dot_general(qh[h], group(kb, h), _NT, preferred_element_type=jnp.float32)
                for h in heads]

    def weights(y, carry, strict):
        y = jnp.minimum(y, SB_SCORE_CLAMP)
        u = jnp.log2(1.0 + jnp.exp2(y))
        if strict is not None:
            u = jnp.where(strict, u, 0.0)
        hi, lo = _split_bf16(u)
        incl = jnp.dot(jnp.concatenate([hi, lo], axis=1), tri2,
                       preferred_element_type=jnp.float32)
        a = jnp.exp2(y + carry - incl)
        if strict is not None:
            a = jnp.where(strict, a, 0.0)
        return a.astype(jnp.bfloat16), carry - incl[:, :1]

    def step(j, strict, first):
        y = scores(j)
        if not first:
            vb = tile(v_ref, j + 1)
            for h in heads:
                acc_ref[h] += jnp.dot(a_ref[h], group(vb, h),
                                      preferred_element_type=jnp.float32)
        carries = []
        for h in heads:
            carry = jnp.zeros((tq, 1), jnp.float32) if first else c_ref[h]
            a_ref[h], carry = weights(y[h], carry, strict)
            c_ref[h] = carry
            carries.append(carry)
        return jnp.max(functools.reduce(jnp.maximum, carries))

    acc_ref[...] = jnp.zeros_like(acc_ref)
    row = lax.broadcasted_iota(jnp.int32, (tq, tq), 0)
    col = lax.broadcasted_iota(jnp.int32, (tq, tq), 1)
    stick = step(i, col < row, True)

    def cond(state):
        t, stick = state
        return jnp.logical_and(t < i, stick > UNDERFLOW_LOG2)

    def body(state):
        t, _ = state
        return t + 1, step(i - 1 - t, None, False)

    done, _ = lax.while_loop(cond, body, (jnp.int32(0), stick))
    vb = tile(v_ref, i - done)
    out = [acc_ref[h] + jnp.dot(a_ref[h], group(vb, h), preferred_element_type=jnp.float32)
           for h in heads]
    res = []
    for g in range(SB_HEADS // SB_GROUP):
        r = out[g * SB_GROUP]
        for h in range(1, SB_GROUP):
            r = jnp.where((lane // HEAD_DIM) == h, out[g * SB_GROUP + h], r)
        res.append(r)
    o_ref[0] = jnp.concatenate(res, axis=1).astype(o_ref.dtype)


def _stick_breaking(qkvg, batch, seq):
    tq = ATT_BLOCK
    width = SB_HEADS * HEAD_DIM
    blocks = D_MODEL // width
    tri = (np.arange(tq)[:, None] >= np.arange(tq)[None, :]).astype(np.float32)
    tri2 = jnp.asarray(np.concatenate([tri, tri], axis=0), jnp.bfloat16)
    return pl.pallas_call(
        _sb_kernel,
        grid=(batch, blocks, seq // tq),
        in_specs=[pl.BlockSpec((1, tq, width), lambda b, h, i: (b, i, h)),
                  pl.BlockSpec((1, seq, width), lambda b, h, i: (b, 0, blocks + h)),
                  pl.BlockSpec((1, seq, width), lambda b, h, i: (b, 0, 2 * blocks + h)),
                  pl.BlockSpec((2 * tq, tq), lambda b, h, i: (0, 0))],
        out_specs=pl.BlockSpec((1, tq, width), lambda b, h, i: (b, i, h)),
        out_shape=jax.ShapeDtypeStruct((batch, seq, D_MODEL), jnp.bfloat16),
        scratch_shapes=[pltpu.VMEM((SB_HEADS, tq, SB_GROUP * HEAD_DIM), jnp.float32),
                        pltpu.VMEM((SB_HEADS, tq, 1), jnp.float32),
                        pltpu.VMEM((SB_HEADS, tq, tq), jnp.bfloat16)],
        compiler_params=_params("parallel", "parallel", "arbitrary"),
        name="stick_breaking",
    )(qkvg, qkvg, qkvg, tri2)


def _diff_kernel(q_ref, k_ref, vt_ref, lam_ref, sg_ref, o_ref, acc_ref, m_ref, alpha_ref, s_ref,
                 p_ref, *, lam_init):
    tq = tk = DIFF_BLOCK
    i = pl.program_id(2)
    pairs = tuple(range(2 * DIFF_HEADS))
    lane = lax.broadcasted_iota(jnp.int32, (1, LANES), 1)

    def head_lanes(c):
        return slice(c // 2 * LANES, (c // 2 + 1) * LANES)

    qs = q_ref[0]
    qc = [jnp.where((lane // HEAD_DIM) == c % 2, qs[:, head_lanes(c)], 0).astype(qs.dtype)
          for c in pairs]
    ones = jnp.ones((ONES_ROWS, tk), jnp.bfloat16)

    def scores(j, cs):
        keys = pl.ds(pl.multiple_of(j * tk, tk), tk)
        return [lax.dot_general(k_ref[0, keys, head_lanes(c)], qc[c], _NT,
                                preferred_element_type=jnp.float32) for c in cs]

    def accumulate(acc, alpha, p, j, cs):
        keys = pl.ds(pl.multiple_of(j * tk, tk), tk)
        out = []
        for n, c in enumerate(cs):
            vt1 = jnp.concatenate([vt_ref[0, head_lanes(c), keys], ones], axis=0)
            out.append(alpha[n] * acc[n] + jnp.dot(vt1, p[n], preferred_element_type=jnp.float32))
        return out

    def softmax_tile(s, m, causal):
        p, alpha, m_out = [], [], []
        for sc, m_old in zip(s, m):
            if causal is not None:
                sc = jnp.where(causal, sc, NEG_BIG)
            m_new = jnp.maximum(m_old, jnp.max(sc, axis=0, keepdims=True))
            alpha.append(jnp.exp2(m_old - m_new))
            p.append(jnp.exp2(sc - m_new).astype(jnp.bfloat16))
            m_out.append(m_new)
        return p, alpha, m_out

    def load(ref, cs):
        return [ref[c] for c in cs]

    def store(ref, vals, cs):
        for c, val in zip(cs, vals):
            ref[c] = val

    def step(j, cs):
        s_next = scores(jnp.maximum(j - 1, 0), cs)
        acc = accumulate(load(acc_ref, cs), load(alpha_ref, cs), load(p_ref, cs), j + 1, cs)
        p, alpha, m = softmax_tile(load(s_ref, cs), load(m_ref, cs), None)
        store(acc_ref, acc, cs), store(p_ref, p, cs), store(alpha_ref, alpha, cs)
        store(m_ref, m, cs), store(s_ref, s_next, cs)

    key = lax.broadcasted_iota(jnp.int32, (tk, tq), 0)
    qry = lax.broadcasted_iota(jnp.int32, (tk, tq), 1)
    s_diag = scores(i, pairs)
    s_next = scores(jnp.maximum(i - 1, 0), pairs)
    m0 = [jnp.full((1, tq), NEG_BIG, jnp.float32) for c in pairs]
    p, alpha, m = softmax_tile(s_diag, m0, key <= qry)
    store(p_ref, p, pairs), store(alpha_ref, alpha, pairs), store(m_ref, m, pairs)
    store(s_ref, s_next, pairs)
    acc_ref[...] = jnp.zeros_like(acc_ref)

    for h in range(DIFF_HEADS):
        def body(t, _, cs=(2 * h, 2 * h + 1)):
            step(i - 1 - t, cs)
            return 0

        lax.fori_loop(0, i, body, 0)
    acc = accumulate(load(acc_ref, pairs), load(alpha_ref, pairs), load(p_ref, pairs), 0, pairs)

    lp = lam_ref[...]
    lam = (jnp.exp(jnp.sum(lp[0:1] * lp[1:2], axis=-1, keepdims=True))
           - jnp.exp(jnp.sum(lp[2:3] * lp[3:4], axis=-1, keepdims=True)) + lam_init)
    num = [acc[c][:LANES] for c in pairs]
    den = [acc[c][LANES:LANES + 1] for c in pairs]
    outs = []
    for h in range(DIFF_HEADS):
        c1, c2 = 2 * h, 2 * h + 1
        ot = num[c1] / den[c1] - lam * (num[c2] / den[c2])
        ot = ot * lax.rsqrt(jnp.mean(ot * ot, axis=0, keepdims=True) + EPS)
        outs.append(ot.T * sg_ref[...] * (1.0 - lam_init))
    o_ref[0] = jnp.concatenate(outs, axis=1).astype(o_ref.dtype)


def _diff_attention(proj, vt, lam_params, subln_g, lam_init, batch, seq):
    tq = DIFF_BLOCK
    width = DIFF_HEADS * LANES
    blocks = D_MODEL // width
    rows = LANES + ONES_ROWS
    pairs = 2 * DIFF_HEADS
    return pl.pallas_call(
        functools.partial(_diff_kernel, lam_init=lam_init),
        grid=(batch, blocks, seq // tq),
        in_specs=[pl.BlockSpec((1, tq, width), lambda b, h, i: (b, i, 2 * blocks + h)),
                  pl.BlockSpec((1, seq, width), lambda b, h, i: (b, 0, h)),
                  pl.BlockSpec((1, width, seq), lambda b, h, i: (b, h, 0)),
                  pl.BlockSpec((4, HEAD_DIM), lambda b, h, i: (0, 0)),
                  pl.BlockSpec((1, LANES), lambda b, h, i: (0, 0))],
        out_specs=pl.BlockSpec((1, tq, width), lambda b, h, i: (b, i, h)),
        out_shape=jax.ShapeDtypeStruct((batch, seq, D_MODEL), jnp.bfloat16),
        scratch_shapes=[pltpu.VMEM((pairs, rows, tq), jnp.float32),
                        pltpu.VMEM((pairs, 1, tq), jnp.float32),
                        pltpu.VMEM((pairs, 1, tq), jnp.float32),
                        pltpu.VMEM((pairs, tq, tq), jnp.float32),
                        pltpu.VMEM((pairs, tq, tq), jnp.bfloat16)],
        compiler_params=_params("parallel", "parallel", "arbitrary"),
        name="diff_attention",
    )(proj, proj, vt, lam_params, subln_g.reshape(1, LANES))


def kernel(x, positions, a_norm_g, a_w_in, a_w_out, kv_norm_g, w_kv, k_norm_g, b_norm_g,
           b_w_in, b_q_norm_g, b_lambda, b_subln_g, b_w_out):
    batch, seq, d = x.shape
    assert d == D_MODEL and seq % ATT_BLOCK == 0 and seq % DIFF_BLOCK == 0
    assert seq % PROJ_ROWS == 0 and (batch * seq) % OUT_PROJ_ROWS == 0
    assert a_norm_g.shape[0] == 1 and b_norm_g.shape[0] == 1
    bf16 = jnp.bfloat16
    x2 = x.reshape(batch * seq, d)

    qkvg = _proj_a(x2, a_norm_g[0], a_w_in[0].astype(bf16))
    o = _stick_breaking(qkvg.reshape(batch, seq, 4 * d), batch, seq)

    w_b = jnp.concatenate([w_kv, b_w_in[0]], axis=1).astype(bf16)
    g2 = jnp.stack([kv_norm_g, b_norm_g[0]]).reshape(2, 1, d)
    ng2 = jnp.stack([jnp.tile(k_norm_g, d // HEAD_DIM),
                     jnp.tile(b_q_norm_g[0], d // HEAD_DIM)]).reshape(2, 1, d)
    x2, proj, vt = _proj_b(qkvg, o.reshape(batch * seq, d), x2, a_w_out[0].astype(bf16), g2, w_b,
                           ng2, positions, batch)
    layer_idx = N_A_LAYERS
    lam_init = 0.8 - 0.6 * math.exp(-0.3 * layer_idx)
    o = _diff_attention(proj.reshape(batch, seq, 4 * d), vt, b_lambda[0], b_subln_g[0],
                        lam_init, batch, seq)
    x2 = _out_proj(proj, 3, o.reshape(batch * seq, d), x2, b_w_out[0].astype(bf16))
    return x2.reshape(batch, seq, d)
```
